```python
import math
import jax, jax.numpy as jnp
from jax import lax
import numpy as np

D_MODEL = 1024
BATCH = 8
SEQ = 2048
DEPTH = 4

CONV_A_WIDTH = 512
CONV_A_K = 3
CONV_B_WIDTH = 512
CONV_B_K = 31
N_HEADS = 8
N_KV = 2
HEAD_DIM = 64
GROUP = N_HEADS // N_KV
ATTN_WIDTH = N_HEADS * HEAD_DIM
KV_WIDTH = N_KV * HEAD_DIM
N_BRANCH = 3
CMP_BLOCK = 32
CMP_STRIDE = 16
CMP_HIDDEN = 256
SLC_BLOCK = 64
SLC_TOPN = 16
SLC_QCHUNK = 64
WINDOW = 512
WIN_QBLOCK = 128
ROPE_THETA = 10000.0
D_FF = -(-8 * D_MODEL // (3 * 256)) * 256
EPS = 1e-6
NEG = -1e30
IN_SPLITS = [CONV_A_WIDTH] * 3 + [2 * CONV_B_WIDTH] + [ATTN_WIDTH] + [KV_WIDTH] * 6 + [N_HEADS * N_BRANCH] + [D_MODEL] * 3
IN_WIDTH = sum(IN_SPLITS)

kernel_name = "hybrid_conv_conformer_nsa_block"


def rms_norm(x, g):
    xf = x.astype(jnp.float32)
    y = xf * lax.rsqrt(jnp.mean(xf * xf, axis=-1, keepdims=True) + EPS)
    return (y * g.astype(jnp.float32)).astype(x.dtype)


def layer_norm(x, g, b):
    xf = x.astype(jnp.float32)
    mu = jnp.mean(xf, axis=-1, keepdims=True)
    var = jnp.mean(jnp.square(xf - mu), axis=-1, keepdims=True)
    y = (xf - mu) * lax.rsqrt(var + EPS)
    return (y * g.astype(jnp.float32) + b.astype(jnp.float32)).astype(x.dtype)


def depthwise_causal_conv(x, w):
    k = w.shape[0]
    return lax.conv_general_dilated(
        x, w.astype(x.dtype)[:, None, :], window_strides=(1,), padding=[(k - 1, 0)],
        dimension_numbers=('NWC', 'WIO', 'NWC'), feature_group_count=x.shape[-1])


def rope_tables(s):
    pos = jnp.arange(s, dtype=jnp.float32)
    inv = 1.0 / (ROPE_THETA ** (jnp.arange(0, HEAD_DIM, 2, dtype=jnp.float32) / HEAD_DIM))
    ang = pos[:, None] * inv[None, :]
    return jnp.cos(ang), jnp.sin(ang)


def apply_rope(x, cos, sin):
    xf = x.astype(jnp.float32)
    half = x.shape[-1] // 2
    x1, x2 = xf[..., :half], xf[..., half:]
    return jnp.concatenate([x1 * cos - x2 * sin, x2 * cos + x1 * sin], axis=-1).astype(x.dtype)


def masked_softmax(s, mask):
    s = jnp.where(mask, s.astype(jnp.float32), NEG)
    p = jax.nn.softmax(s, axis=-1)
    return jnp.where(mask, p, 0.0)


def short_conv_mixer(c_gate, b_gate, h, conv_w, w_out):
    v = depthwise_causal_conv(c_gate * h, conv_w)
    return (b_gate * v) @ w_out


def conformer_conv_mixer(z, conv_w, conv_b, ln_g, ln_b, w_out):
    a, g = jnp.split(z, 2, axis=-1)
    u = a * jax.nn.sigmoid(g)
    u = depthwise_causal_conv(u, conv_w) + conv_b
    u = jax.nn.silu(layer_norm(u, ln_g, ln_b))
    return u @ w_out


def nsa_mixer(q, kvs, gate_logits, q_g, k_g, cmp_pe, cmp_w1, cmp_b1, cmp_w2, w_out, cos, sin):
    b, s, _ = q.shape
    dt = q.dtype
    scale = HEAD_DIM ** -0.5
    t = jnp.arange(s)
    q = q.reshape(b, s, N_KV, GROUP, HEAD_DIM).transpose(0, 2, 3, 1, 4)
    q = apply_rope(rms_norm(q, q_g), cos, sin)

    def heads(u):
        return u.reshape(b, s, N_KV, HEAD_DIM).transpose(0, 2, 1, 3)

    k_cmp, v_cmp, k_slc, v_slc, k_win, v_win = [heads(u) for u in kvs]
    k_cmp = apply_rope(rms_norm(k_cmp, k_g[0]), cos, sin)
    k_slc = apply_rope(rms_norm(k_slc, k_g[1]), cos, sin)
    k_win = apply_rope(rms_norm(k_win, k_g[2]), cos, sin)

    n_cmp = (s - CMP_BLOCK) // CMP_STRIDE + 1
    blk_idx = jnp.arange(n_cmp)[:, None] * CMP_STRIDE + jnp.arange(CMP_BLOCK)[None, :]

    def compress(u, pe, w1, b1, w2):
        blocks = u[:, :, blk_idx] + pe
        flat = blocks.reshape(b, N_KV, n_cmp, CMP_BLOCK * HEAD_DIM)
        return jax.nn.gelu(flat @ w1 + b1) @ w2

    kc = compress(k_cmp, cmp_pe[0], cmp_w1[0], cmp_b1[0], cmp_w2[0])
    vc = compress(v_cmp, cmp_pe[1], cmp_w1[1], cmp_b1[1], cmp_w2[1])
    block_end = jnp.arange(n_cmp) * CMP_STRIDE + CMP_BLOCK - 1
    cmask = block_end[None, :] <= t[:, None]
    sc = jnp.einsum('bgrsd,bgcd->bgrsc', q, kc) * scale
    p_cmp = masked_softmax(sc, cmask)
    o_cmp = jnp.einsum('bgrsc,bgcd->bgrsd', p_cmp.astype(dt), vc)

    n_sel = s // SLC_BLOCK
    n_top = min(SLC_TOPN, n_sel)
    cmp_start = jnp.arange(n_cmp) * CMP_STRIDE
    sel_start = jnp.arange(n_sel) * SLC_BLOCK
    overlap = ((cmp_start[:, None] < sel_start[None, :] + SLC_BLOCK)
               & (cmp_start[:, None] + CMP_BLOCK > sel_start[None, :])).astype(jnp.float32)
    importance = jnp.einsum('bgrsc,cj->bgsj', p_cmp, overlap)
    cur = t // SLC_BLOCK
    j = jnp.arange(n_sel)
    valid = j[None, :] <= cur[:, None]
    forced = (j[None, :] == 0) | (j[None, :] == cur[:, None]) | (j[None, :] == cur[:, None] - 1)
    score = jnp.where(forced, jnp.inf, jnp.where(valid, importance, -jnp.inf))
    top_val, top_idx = lax.top_k(score, n_top)
    top_ok = top_val > -jnp.inf

    n_q = s // SLC_QCHUNK
    kb = k_slc.reshape(b, N_KV, n_sel, SLC_BLOCK, HEAD_DIM)
    vb = v_slc.reshape(b, N_KV, n_sel, SLC_BLOCK, HEAD_DIM)
    q_ch = q.reshape(b, N_KV, GROUP, n_q, SLC_QCHUNK, HEAD_DIM).transpose(3, 0, 1, 2, 4, 5)
    idx_ch = top_idx.reshape(b, N_KV, n_q, SLC_QCHUNK, n_top).transpose(2, 0, 1, 3, 4)
    ok_ch = top_ok.reshape(b, N_KV, n_q, SLC_QCHUNK, n_top).transpose(2, 0, 1, 3, 4)
    t_ch = t.reshape(n_q, SLC_QCHUNK)
    gather = jax.vmap(jax.vmap(lambda blocks, ix: blocks[ix]))
    n_keys = n_top * SLC_BLOCK

    def slc_chunk(args):
        qc, ic, okc, tc = args
        kg = gather(kb, ic).reshape(b, N_KV, SLC_QCHUNK, n_keys, HEAD_DIM)
        vg = gather(vb, ic).reshape(b, N_KV, SLC_QCHUNK, n_keys, HEAD_DIM)
        kpos = (ic[..., None] * SLC_BLOCK + jnp.arange(SLC_BLOCK)).reshape(b, N_KV, SLC_QCHUNK, n_keys)
        m = jnp.repeat(okc, SLC_BLOCK, axis=-1) & (kpos <= tc[None, None, :, None])
        ss = jnp.einsum('bgrqd,bgqkd->bgrqk', qc, kg) * scale
        pp = masked_softmax(ss, m[:, :, None])
        return jnp.einsum('bgrqk,bgqkd->bgrqd', pp.astype(dt), vg)

    o_slc = lax.map(slc_chunk, (q_ch, idx_ch, ok_ch, t_ch))
    o_slc = o_slc.transpose(1, 2, 3, 0, 4, 5).reshape(b, N_KV, GROUP, s, HEAD_DIM)

    n_b = s // WIN_QBLOCK
    band = WIN_QBLOCK + WINDOW
    band_idx = jnp.arange(n_b)[:, None] * WIN_QBLOCK + jnp.arange(band)[None, :]
    pad = ((0, 0), (0, 0), (WINDOW, 0), (0, 0))
    kw = jnp.pad(k_win, pad)[:, :, band_idx]
    vw = jnp.pad(v_win, pad)[:, :, band_idx]
    kpos = band_idx - WINDOW
    tq = t.reshape(n_b, WIN_QBLOCK)
    wmask = ((kpos[:, None, :] >= 0) & (kpos[:, None, :] <= tq[:, :, None])
             & (kpos[:, None, :] > tq[:, :, None] - WINDOW))
    qw = q.reshape(b, N_KV, GROUP, n_b, WIN_QBLOCK, HEAD_DIM)
    sw = jnp.einsum('bgrnqd,bgnkd->bgrnqk', qw, kw) * scale
    pw = masked_softmax(sw, wmask)
    o_win = jnp.einsum('bgrnqk,bgnkd->bgrnqd', pw.astype(dt), vw).reshape(b, N_KV, GROUP, s, HEAD_DIM)

    g = jax.nn.sigmoid(gate_logits).reshape(b, s, N_KV, GROUP, N_BRANCH).transpose(0, 2, 3, 1, 4)
    o = g[..., 0:1] * o_cmp + g[..., 1:2] * o_slc + g[..., 2:3] * o_win
    o = o.astype(dt).transpose(0, 3, 1, 2, 4).reshape(b, s, ATTN_WIDTH)
    return o @ w_out


def setup_inputs(seed: int = 0) -> dict:
    key = jax.random.key(seed)
    ks = jax.random.split(key, 24)
    L = DEPTH

    def nrm(k, shape, scale):
        return jax.random.normal(k, shape, jnp.float32) * scale

    return {
        'x': nrm(ks[0], (BATCH, SEQ, D_MODEL), 1.0),
        'norm1_g': 1.0 + nrm(ks[1], (L, D_MODEL), 0.05),
        'w_in': nrm(ks[2], (L, D_MODEL, IN_WIDTH), D_MODEL ** -0.5),
        'a_conv_w': nrm(ks[3], (L, CONV_A_K, CONV_A_WIDTH), CONV_A_K ** -0.5),
        'a_w_out': nrm(ks[4], (L, CONV_A_WIDTH, D_MODEL), CONV_A_WIDTH ** -0.5),
        'b_conv_w': nrm(ks[5], (L, CONV_B_K, CONV_B_WIDTH), CONV_B_K ** -0.5),
        'b_conv_b': nrm(ks[6], (L, CONV_B_WIDTH), 0.02),
        'b_ln_g': 1.0 + nrm(ks[7], (L, CONV_B_WIDTH), 0.05),
        'b_ln_b': nrm(ks[8], (L, CONV_B_WIDTH), 0.02),
        'b_w_out': nrm(ks[9], (L, CONV_B_WIDTH, D_MODEL), CONV_B_WIDTH ** -0.5),
        'q_norm_g': 1.0 + nrm(ks[10], (L, HEAD_DIM), 0.05),
        'k_norm_g': 1.0 + nrm(ks[11], (L, N_BRANCH, HEAD_DIM), 0.05),
        'cmp_pe': nrm(ks[12], (L, 2, CMP_BLOCK, HEAD_DIM), 0.02),
        'cmp_w1': nrm(ks[13], (L, 2, CMP_BLOCK * HEAD_DIM, CMP_HIDDEN), (CMP_BLOCK * HEAD_DIM) ** -0.5),
        'cmp_b1': nrm(ks[14], (L, 2, CMP_HIDDEN), 0.02),
        'cmp_w2': nrm(ks[15], (L, 2, CMP_HIDDEN, HEAD_DIM), CMP_HIDDEN ** -0.5),
        'nsa_w_out': nrm(ks[16], (L, ATTN_WIDTH, D_MODEL), ATTN_WIDTH ** -0.5),
        'w_o': nrm(ks[17], (L, D_MODEL, D_MODEL), 0.5 * D_MODEL ** -0.5),
        'norm2_g': 1.0 + nrm(ks[18], (L, D_MODEL), 0.05),
        'ffn_w13': nrm(ks[19], (L, D_MODEL, 2 * D_FF), D_MODEL ** -0.5),
        'ffn_w2': nrm(ks[20], (L, D_FF, D_MODEL), 0.5 * D_FF ** -0.5),
    }


def reference(x, norm1_g, w_in, a_conv_w, a_w_out, b_conv_w, b_conv_b, b_ln_g, b_ln_b, b_w_out,
              q_norm_g, k_norm_g, cmp_pe, cmp_w1, cmp_b1, cmp_w2, nsa_w_out, w_o, norm2_g,
              ffn_w13, ffn_w2):
    s = x.shape[1]
    cos, sin = rope_tables(s)
    offsets = [sum(IN_SPLITS[:i + 1]) for i in range(len(IN_SPLITS) - 1)]
    for l in range(DEPTH):
        h = rms_norm(x, norm1_g[l])
        parts = jnp.split(h @ w_in[l], offsets, axis=-1)
        a_c, a_b, a_h, b_z, q = parts[0], parts[1], parts[2], parts[3], parts[4]
        kvs = parts[5:11]
        nsa_gates = parts[11]
        g_a, g_b, g_c = [jax.nn.sigmoid(p) for p in parts[12:15]]
        y_a = short_conv_mixer(a_c, a_b, a_h, a_conv_w[l], a_w_out[l])
        y_b = conformer_conv_mixer(b_z, b_conv_w[l], b_conv_b[l], b_ln_g[l], b_ln_b[l], b_w_out[l])
        y_c = nsa_mixer(q, kvs, nsa_gates, q_norm_g[l], k_norm_g[l], cmp_pe[l], cmp_w1[l], cmp_b1[l],
                        cmp_w2[l], nsa_w_out[l], cos, sin)
        mixed = g_a * y_a + g_b * y_b + g_c * y_c
        x = x + (mixed @ w_o[l]).astype(x.dtype)
        h2 = rms_norm(x, norm2_g[l])
        u, v = jnp.split(h2 @ ffn_w13[l], 2, axis=-1)
        x = x + ((jax.nn.silu(u) * v) @ ffn_w2[l]).astype(x.dtype)
    return x
```

```python
import functools

import jax
import jax.numpy as jnp
from jax import lax
from jax.experimental import pallas as pl
from jax.experimental.pallas import tpu as pltpu

F32 = jnp.float32
BF16 = jnp.bfloat16

D_MODEL = 1024
DEPTH = 4
CONV_A_WIDTH = 512
CONV_A_K = 3
CONV_B_WIDTH = 512
CONV_B_K = 31
N_HEADS = 8
N_KV = 2
HEAD_DIM = 64
GROUP = N_HEADS // N_KV
ATTN_WIDTH = N_HEADS * HEAD_DIM
KV_WIDTH = N_KV * HEAD_DIM
N_BRANCH = 3
CMP_BLOCK = 32
CMP_STRIDE = 16
CMP_HIDDEN = 256
SLC_BLOCK = 64
SLC_TOPN = 16
WINDOW = 512
ROPE_THETA = 10000.0
D_FF = -(-8 * D_MODEL // (3 * 256)) * 256
EPS = 1e-6
NEG = -1e30
SCALE = HEAD_DIM ** -0.5

N_GATE = N_HEADS * N_BRANCH
GATE_COL = 3 * CONV_A_WIDTH + 2 * CONV_B_WIDTH + ATTN_WIDTH + 6 * KV_WIDTH
GATE_PAD = 256
PROJ_WIDTH = GATE_COL + GATE_PAD + 3 * D_MODEL

VMEM_LIMIT = 48 * 1024 * 1024
HALO = 32
CONV_ROWS = 32


def _cparams(*sem):
    return pltpu.CompilerParams(dimension_semantics=sem, vmem_limit_bytes=VMEM_LIMIT)


def _rmsnorm_kernel(x_ref, g_ref, h_ref):
    x = x_ref[...]
    ms = jnp.mean(x * x, axis=-1, keepdims=True)
    h_ref[...] = (x * lax.rsqrt(ms + EPS) * g_ref[...]).astype(h_ref.dtype)


def _rmsnorm(x, g, tm=1024):
    t, d = x.shape
    return pl.pallas_call(
        _rmsnorm_kernel,
        out_shape=jax.ShapeDtypeStruct((t, d), BF16),
        grid=(t // tm,),
        in_specs=[pl.BlockSpec((tm, d), lambda i: (i, 0)), pl.BlockSpec((1, d), lambda i: (0, 0))],
        out_specs=pl.BlockSpec((tm, d), lambda i: (i, 0)),
        compiler_params=_cparams("parallel"),
        name="rmsnorm",
    )(x, g)


def _mm_kernel(a_ref, w_ref, o_ref):
    o_ref[...] = jnp.dot(a_ref[...], w_ref[...], preferred_element_type=F32).astype(o_ref.dtype)


def _matmul(a, w, out_dtype, tm=1024, tn=1024):
    m, k = a.shape
    n = w.shape[1]
    return pl.pallas_call(
        _mm_kernel,
        out_shape=jax.ShapeDtypeStruct((m, n), out_dtype),
        grid=(m // tm, n // tn),
        in_specs=[pl.BlockSpec((tm, k), lambda i, j: (i, 0)), pl.BlockSpec((k, tn), lambda i, j: (0, j))],
        out_specs=pl.BlockSpec((tm, tn), lambda i, j: (i, j)),
        compiler_params=_cparams("parallel", "parallel"),
        name="in_proj",
    )(a, w)


def _conv_kernel(ac_ref, ab_ref, ah_ref, ba_ref, bg_ref, ga_ref, gb_ref,
                 wa_ref, wb_ref, bb_ref, lng_ref, lnb_ref, wao_ref, wbo_ref,
                 out_ref, ua_ext, ub_ext, vb_scr):
    ts = out_ref.shape[1]
    width = ua_ext.shape[1]

    @pl.when(pl.program_id(1) == 0)
    def _():
        ua_ext[0:HALO, :] = jnp.zeros((HALO, width), F32)
        ub_ext[0:HALO, :] = jnp.zeros((HALO, width), F32)

    ua_ext[HALO:HALO + ts, :] = ac_ref[0].astype(F32) * ah_ref[0].astype(F32)
    ub_ext[HALO:HALO + ts, :] = ba_ref[0].astype(F32) * jax.nn.sigmoid(bg_ref[0].astype(F32))

    wa = wa_ref[...]
    va = wa[0:1] * ua_ext[HALO - 2:HALO - 2 + ts, :]
    va = va + wa[1:2] * ua_ext[HALO - 1:HALO - 1 + ts, :]
    va = va + wa[2:3] * ua_ext[HALO:HALO + ts, :]
    ya = jnp.dot((ab_ref[0].astype(F32) * va).astype(BF16), wao_ref[...], preferred_element_type=F32)

    wb = wb_ref[...]
    bias = jnp.broadcast_to(bb_ref[...], (CONV_ROWS, width))
    for c in range(ts // CONV_ROWS):
        acc = bias
        for k in range(CONV_B_K):
            r0 = HALO - (CONV_B_K - 1) + k + c * CONV_ROWS
            acc = acc + wb[k:k + 1] * ub_ext[r0:r0 + CONV_ROWS, :]
        vb_scr[c * CONV_ROWS:(c + 1) * CONV_ROWS, :] = acc
    u = vb_scr[...]
    mu = jnp.mean(u, axis=-1, keepdims=True)
    var = jnp.mean(jnp.square(u - mu), axis=-1, keepdims=True)
    y = (u - mu) * lax.rsqrt(var + EPS)
    y = y * lng_ref[...] + lnb_ref[...]
    y = y * jax.nn.sigmoid(y)
    yb = jnp.dot(y.astype(BF16), wbo_ref[...], preferred_element_type=F32)

    out_ref[0] = (jax.nn.sigmoid(ga_ref[0].astype(F32)) * ya
                  + jax.nn.sigmoid(gb_ref[0].astype(F32)) * yb)

    ua_ext[0:HALO, :] = ua_ext[ts:ts + HALO, :]
    ub_ext[0:HALO, :] = ub_ext[ts:ts + HALO, :]


def _conv_mixers(proj, wa, wb, bb, lng, lnb, wao, wbo, ts=512):
    b, s, _ = proj.shape
    w = CONV_A_WIDTH
    gate_blk = (GATE_COL + GATE_PAD) // D_MODEL

    def col(width, idx):
        return pl.BlockSpec((1, ts, width), lambda bi, si: (bi, si, idx))

    def full(arr):
        return pl.BlockSpec(arr.shape, lambda bi, si: (0,) * arr.ndim)

    return pl.pallas_call(
        _conv_kernel,
        out_shape=jax.ShapeDtypeStruct((b, s, D_MODEL), F32),
        grid=(b, s // ts),
        in_specs=[col(w, 0), col(w, 1), col(w, 2), col(w, 3), col(w, 4),
                  col(D_MODEL, gate_blk), col(D_MODEL, gate_blk + 1),
                  full(wa), full(wb), full(bb), full(lng), full(lnb), full(wao), full(wbo)],
        out_specs=pl.BlockSpec((1, ts, D_MODEL), lambda bi, si: (bi, si, 0)),
        scratch_shapes=[pltpu.VMEM((HALO + ts, w), F32), pltpu.VMEM((HALO + ts, w), F32),
                        pltpu.VMEM((ts, w), F32)],
        compiler_params=_cparams("parallel", "arbitrary"),
        name="conv_mixers",
    )(proj, proj, proj, proj, proj, proj, proj, wa, wb, bb, lng, lnb, wao, wbo)


def _prep_kernel(q_ref, kv_ref, gt_ref, cos_ref, sin_ref, qg_ref, kg_ref, mblk_ref,
                 qh_ref, kh_ref, vh_ref, go_ref):
    ts = q_ref.shape[1]
    cos = cos_ref[...]
    sin = sin_ref[...]
    mblk = mblk_ref[...]
    lane = lax.broadcasted_iota(jnp.int32, (ts, 128), 1)
    first_half = (lane & (HEAD_DIM - 1)) < HEAD_DIM // 2

    def norm_rope(x, g):
        xx = x * x
        hi = xx.astype(BF16)
        lo = (xx - hi.astype(F32)).astype(BF16)
        ms = (jnp.dot(hi, mblk, preferred_element_type=F32)
              + jnp.dot(lo, mblk, preferred_element_type=F32))
        y = x * lax.rsqrt(ms + EPS) * g
        partner = jnp.where(first_half, pltpu.roll(y, 128 - HEAD_DIM // 2, 1), pltpu.roll(y, HEAD_DIM // 2, 1))
        return y * cos + partner * sin

    qg = qg_ref[...]
    for j in range(ATTN_WIDTH // 128):
        o = norm_rope(q_ref[0, :, 128 * j:128 * (j + 1)].astype(F32), qg) * SCALE
        qh_ref[0, 2 * j] = o[:, :HEAD_DIM].astype(BF16)
        qh_ref[0, 2 * j + 1] = o[:, HEAD_DIM:].astype(BF16)
    for i in range(3):
        k = norm_rope(kv_ref[0, :, 256 * i:256 * i + 128].astype(F32), kg_ref[i:i + 1, :])
        kh_ref[0, i, 0] = k[:, :HEAD_DIM].astype(BF16)
        kh_ref[0, i, 1] = k[:, HEAD_DIM:].astype(BF16)
        v = kv_ref[0, :, 256 * i + 128:256 * i + 256].astype(F32)
        vh_ref[0, i, 0] = v[:, :HEAD_DIM].astype(BF16)
        vh_ref[0, i, 1] = v[:, HEAD_DIM:].astype(BF16)
    sg = jax.nn.sigmoid(gt_ref[0].astype(F32))
    go_ref[0, 0] = sg
    go_ref[0, 1] = pltpu.roll(sg, 128 - GROUP * N_BRANCH, 1)


def _attn_prep(proj, cos128, sin128, qg128, kg128, mblk, ts=512):
    b, s, _ = proj.shape

    def full(arr):
        return pl.BlockSpec(arr.shape, lambda bi, si: (0,) * arr.ndim)

    q_blk = (3 * CONV_A_WIDTH + 2 * CONV_B_WIDTH) // ATTN_WIDTH
    kv_blk = (q_blk + 1) * ATTN_WIDTH // (6 * KV_WIDTH)
    gt_blk = GATE_COL // 128
    return pl.pallas_call(
        _prep_kernel,
        out_shape=(jax.ShapeDtypeStruct((b, N_HEADS, s, HEAD_DIM), BF16),
                   jax.ShapeDtypeStruct((b, 3, N_KV, s, HEAD_DIM), BF16),
                   jax.ShapeDtypeStruct((b, 3, N_KV, s, HEAD_DIM), BF16),
                   jax.ShapeDtypeStruct((b, N_KV, s, 128), F32)),
        grid=(b, s // ts),
        in_specs=[pl.BlockSpec((1, ts, ATTN_WIDTH), lambda bi, si: (bi, si, q_blk)),
                  pl.BlockSpec((1, ts, 6 * KV_WIDTH), lambda bi, si: (bi, si, kv_blk)),
                  pl.BlockSpec((1, ts, 128), lambda bi, si: (bi, si, gt_blk)),
                  pl.BlockSpec((ts, 128), lambda bi, si: (si, 0)),
                  pl.BlockSpec((ts, 128), lambda bi, si: (si, 0)),
                  full(qg128), full(kg128), full(mblk)],
        out_specs=(pl.BlockSpec((1, N_HEADS, ts, HEAD_DIM), lambda bi, si: (bi, 0, si, 0)),
                   pl.BlockSpec((1, 3, N_KV, ts, HEAD_DIM), lambda bi, si: (bi, 0, 0, si, 0)),
                   pl.BlockSpec((1, 3, N_KV, ts, HEAD_DIM), lambda bi, si: (bi, 0, 0, si, 0)),
                   pl.BlockSpec((1, N_KV, ts, 128), lambda bi, si: (bi, 0, si, 0))),
        compiler_params=_cparams("parallel", "parallel"),
        name="attn_prep",
    )(proj, proj, proj, cos128, sin128, qg128, kg128, mblk)


def _compress_kernel(k_ref, v_ref, pe_ref, w1_ref, b1_ref, w2_ref, kc_ref, vc_ref):
    nb = k_ref.shape[3]
    half = w1_ref.shape[1] // 2
    row = lax.broadcasted_iota(jnp.int32, (nb, HEAD_DIM), 0)
    for i, (src, dst) in enumerate(((k_ref, kc_ref), (v_ref, vc_ref))):
        u = src[0, 0, 0].astype(F32)
        top = (u + pe_ref[i, 0:1, :]).astype(BF16)
        bot = (u + pe_ref[i, 1:2, :]).astype(BF16)
        a = jnp.dot(top, w1_ref[i, :half, :], preferred_element_type=F32)
        bm = jnp.dot(bot, w1_ref[i, half:, :], preferred_element_type=F32)
        hid = a + pltpu.roll(bm, nb - 1, 0) + b1_ref[i]
        hid = jax.nn.gelu(hid)
        out = jnp.dot(hid.astype(BF16), w2_ref[i], preferred_element_type=F32)
        dst[0, 0] = jnp.where(row < nb - 1, out, 0.0).astype(dst.dtype)


def _compress(kh5, vh5, pe, w1, b1, w2):
    b = kh5.shape[0]
    nb = kh5.shape[3]

    def full(arr):
        return pl.BlockSpec(arr.shape, lambda bi, gi: (0,) * arr.ndim)

    blk = pl.BlockSpec((1, 1, 1, nb, CMP_STRIDE * HEAD_DIM), lambda bi, gi: (bi, 0, gi, 0, 0))
    out_blk = pl.BlockSpec((1, 1, nb, HEAD_DIM), lambda bi, gi: (bi, gi, 0, 0))
    return pl.pallas_call(
        _compress_kernel,
        out_shape=(jax.ShapeDtypeStruct((b, N_KV, nb, HEAD_DIM), BF16),) * 2,
        grid=(b, N_KV),
        in_specs=[blk, blk, full(pe), full(w1), full(b1), full(w2)],
        out_specs=(out_blk, out_blk),
        compiler_params=_cparams("parallel", "parallel"),
        name="compress",
    )(kh5, vh5, pe, w1, b1, w2)


def _attn_kernel(q_ref, kc_ref, vc_ref, ks_ref, vs_ref, kw_ref, vw_ref, gt_ref, o_ref,
                 bias_scr, m_scr, l_scr, acc_scr, *, ck):
    tq = q_ref.shape[2]
    s_len = ks_ref.shape[3]
    nc = kc_ref.shape[2]
    n_sel = s_len // SLC_BLOCK
    rows = GROUP * tq
    t0 = pl.program_id(2) * tq
    nt = (((1,), (1,)), ((), ()))

    q = q_ref[0].reshape(rows, HEAD_DIM)

    tpos = t0 + lax.broadcasted_iota(jnp.int32, (tq, nc), 0)
    cidx = lax.broadcasted_iota(jnp.int32, (tq, nc), 1)
    cmask = ((cidx * CMP_STRIDE + (CMP_BLOCK - 1)) <= tpos) & (cidx < nc - 1)
    sc = lax.dot_general(q, kc_ref[0, 0], nt, preferred_element_type=F32).reshape(GROUP, tq, nc)
    sc = jnp.where(cmask[None], sc, NEG)
    mc = jnp.max(sc, axis=-1, keepdims=True)
    ec = jnp.where(cmask[None], jnp.exp(sc - mc), 0.0)
    dc = jnp.sum(ec, axis=-1, keepdims=True)
    pc = ec / jnp.where(dc > 0.0, dc, 1.0)
    o_cmp = jnp.dot(pc.reshape(rows, nc).astype(BF16), vc_ref[0, 0], preferred_element_type=F32)

    psum = pc[0] + pc[1] + pc[2] + pc[3]
    p_hi = psum.astype(BF16)
    p_lo = (psum - p_hi.astype(F32)).astype(BF16)
    jrow = lax.broadcasted_iota(jnp.int32, (n_sel, nc), 0)
    ccol = lax.broadcasted_iota(jnp.int32, (n_sel, nc), 1)
    ovl = ((ccol * CMP_STRIDE < jrow * SLC_BLOCK + SLC_BLOCK)
           & (ccol * CMP_STRIDE + CMP_BLOCK > jrow * SLC_BLOCK))
    ovl = jnp.where(ovl, 1.0, 0.0).astype(BF16)
    imp = (lax.dot_general(ovl, p_hi, nt, preferred_element_type=F32)
           + lax.dot_general(ovl, p_lo, nt, preferred_element_type=F32))

    jj = lax.broadcasted_iota(jnp.int32, (n_sel, tq), 0)
    cur = (t0 + lax.broadcasted_iota(jnp.int32, (n_sel, tq), 1)) // SLC_BLOCK
    valid = jj <= cur
    forced = (jj == 0) | (jj == cur) | (jj == cur - 1)
    score = jnp.where(forced, jnp.inf, jnp.where(valid, imp, -jnp.inf))
    rank = jnp.zeros((n_sel, tq), F32)
    for jp in range(n_sel):
        other = score[jp:jp + 1, :]
        ahead = (other > score) | ((other == score) & (jj > jp))
        rank = rank + jnp.where(ahead, 1.0, 0.0)
    sel = jnp.where(valid & (rank < float(min(SLC_TOPN, n_sel))), 1.0, 0.0)

    sel_t = sel.T.astype(BF16)
    ej = lax.broadcasted_iota(jnp.int32, (n_sel, s_len), 0)
    ek = lax.broadcasted_iota(jnp.int32, (n_sel, s_len), 1)
    expand = jnp.where(ek // SLC_BLOCK == ej, 1.0, 0.0).astype(BF16)
    selk = jnp.dot(sel_t, expand, preferred_element_type=F32)
    kpos = lax.broadcasted_iota(jnp.int32, (tq, s_len), 1)
    qpos = t0 + lax.broadcasted_iota(jnp.int32, (tq, s_len), 0)
    bias = jnp.where((selk > 0.5) & (kpos <= qpos), 0.0, NEG)
    n_chunks = s_len // ck
    for c in range(n_chunks):
        bias_scr[c] = bias[:, c * ck:(c + 1) * ck]

    m_scr[...] = jnp.full(m_scr.shape, NEG, F32)
    l_scr[...] = jnp.zeros(l_scr.shape, F32)
    acc_scr[...] = jnp.zeros(acc_scr.shape, F32)
    for c in range(n_chunks):
        @pl.when(c * ck < t0 + tq)
        def _():
            k = ks_ref[0, 0, 0, c * ck:(c + 1) * ck, :]
            v = vs_ref[0, 0, 0, c * ck:(c + 1) * ck, :]
            s = lax.dot_general(q, k, nt, preferred_element_type=F32).reshape(GROUP, tq, ck)
            s = s + bias_scr[c][None]
            m_prev = m_scr[...]
            m_new = jnp.maximum(m_prev, jnp.max(s, axis=-1, keepdims=True))
            alpha = jnp.exp(m_prev - m_new)
            p = jnp.exp(s - m_new)
            l_scr[...] = alpha * l_scr[...] + jnp.sum(p, axis=-1, keepdims=True)
            pv = jnp.dot(p.reshape(rows, ck).astype(BF16), v, preferred_element_type=F32)
            acc_scr[...] = alpha * acc_scr[...] + pv.reshape(GROUP, tq, HEAD_DIM)
            m_scr[...] = m_new
    o_slc = acc_scr[...] / l_scr[...]

    wlen = tq + WINDOW
    ws = pl.multiple_of(jnp.maximum(t0 - WINDOW, 0), tq)
    kw = kw_ref[0, 0, 0, pl.ds(ws, wlen), :]
    vw = vw_ref[0, 0, 0, pl.ds(ws, wlen), :]
    sw = lax.dot_general(q, kw, nt, preferred_element_type=F32).reshape(GROUP, tq, wlen)
    wk = ws + lax.broadcasted_iota(jnp.int32, (tq, wlen), 1)
    wq = t0 + lax.broadcasted_iota(jnp.int32, (tq, wlen), 0)
    wmask = (wk <= wq) & (wk > wq - WINDOW)
    sw = jnp.where(wmask[None], sw, NEG)
    mw = jnp.max(sw, axis=-1, keepdims=True)
    ew = jnp.exp(sw - mw)
    pw = ew / jnp.sum(ew, axis=-1, keepdims=True)
    o_win = jnp.dot(pw.reshape(rows, wlen).astype(BF16), vw, preferred_element_type=F32)

    o_cmp = o_cmp.reshape(GROUP, tq, HEAD_DIM)
    o_win = o_win.reshape(GROUP, tq, HEAD_DIM)
    gt = gt_ref[0, 0]
    outs = []
    for r in range(GROUP):
        g0 = gt[:, N_BRANCH * r:N_BRANCH * r + 1]
        g1 = gt[:, N_BRANCH * r + 1:N_BRANCH * r + 2]
        g2 = gt[:, N_BRANCH * r + 2:N_BRANCH * r + 3]
        outs.append(g0 * o_cmp[r] + g1 * o_slc[r] + g2 * o_win[r])
    o_ref[0] = jnp.concatenate(outs, axis=-1).astype(o_ref.dtype)


def _attention(qh, kc, vc, kh, vh, gates, tq=256, ck=512):
    b, _, s, _ = qh.shape
    nc = kc.shape[2]
    rows = GROUP * tq

    def seq(branch):
        return pl.BlockSpec((1, 1, 1, s, HEAD_DIM), lambda bi, gi, qi: (bi, branch, gi, 0, 0))

    cblk = pl.BlockSpec((1, 1, nc, HEAD_DIM), lambda bi, gi, qi: (bi, gi, 0, 0))
    return pl.pallas_call(
        functools.partial(_attn_kernel, ck=ck),
        out_shape=jax.ShapeDtypeStruct((b, s, ATTN_WIDTH), BF16),
        grid=(b, N_KV, s // tq),
        in_specs=[pl.BlockSpec((1, GROUP, tq, HEAD_DIM), lambda bi, gi, qi: (bi, gi, qi, 0)),
                  cblk, cblk, seq(1), seq(1), seq(2), seq(2),
                  pl.BlockSpec((1, 1, tq, 128), lambda bi, gi, qi: (bi, gi, qi, 0))],
        out_specs=pl.BlockSpec((1, tq, GROUP * HEAD_DIM), lambda bi, gi, qi: (bi, qi, gi)),
        scratch_shapes=[pltpu.VMEM((s // ck, tq, ck), F32),
                        pltpu.VMEM((GROUP, tq, 1), F32), pltpu.VMEM((GROUP, tq, 1), F32),
                        pltpu.VMEM((GROUP, tq, HEAD_DIM), F32)],
        compiler_params=_cparams("parallel", "parallel", "arbitrary"),
        name="nsa_attention",
    )(qh, kc, vc, kh, vh, kh, vh, gates)


def _merge_kernel(o_ref, mab_ref, gc_ref, x_ref, wn_ref, wo_ref, g2_ref, xo_ref, h_ref):
    yc = jnp.dot(o_ref[...], wn_ref[...], preferred_element_type=F32)
    mixed = mab_ref[...] + jax.nn.sigmoid(gc_ref[...].astype(F32)) * yc
    x = x_ref[...] + jnp.dot(mixed.astype(BF16), wo_ref[...], preferred_element_type=F32)
    xo_ref[...] = x
    ms = jnp.mean(x * x, axis=-1, keepdims=True)
    h_ref[...] = (x * lax.rsqrt(ms + EPS) * g2_ref[...]).astype(h_ref.dtype)


def _merge(o, mab, proj2d, x, wn, wo, g2, tm=512):
    t, d = x.shape
    gc_blk = (GATE_COL + GATE_PAD) // D_MODEL + 2

    def full(arr):
        return pl.BlockSpec(arr.shape, lambda i: (0,) * arr.ndim)

    return pl.pallas_call(
        _merge_kernel,
        out_shape=(jax.ShapeDtypeStruct((t, d), F32), jax.ShapeDtypeStruct((t, d), BF16)),
        grid=(t // tm,),
        in_specs=[pl.BlockSpec((tm, ATTN_WIDTH), lambda i: (i, 0)),
                  pl.BlockSpec((tm, d), lambda i: (i, 0)),
                  pl.BlockSpec((tm, d), lambda i: (i, gc_blk)),
                  pl.BlockSpec((tm, d), lambda i: (i, 0)),
                  full(wn), full(wo), full(g2)],
        out_specs=(pl.BlockSpec((tm, d), lambda i: (i, 0)), pl.BlockSpec((tm, d), lambda i: (i, 0))),
        compiler_params=_cparams("parallel"),
        name="merge_out_proj",
    )(o, mab, proj2d, x, wn, wo, g2)


def _ffn_kernel(h_ref, x_ref, w1_ref, w3_ref, w2_ref, gn_ref, xo_ref, hn_ref):
    c = pl.program_id(1)

    @pl.when(c == 0)
    def _():
        xo_ref[...] = x_ref[...]

    h = h_ref[...]
    u = jnp.dot(h, w1_ref[...], preferred_element_type=F32)
    v = jnp.dot(h, w3_ref[...], preferred_element_type=F32)
    act = (u * jax.nn.sigmoid(u) * v).astype(BF16)
    xo_ref[...] += jnp.dot(act, w2_ref[...], preferred_element_type=F32)

    @pl.when(c == pl.num_programs(1) - 1)
    def _():
        x = xo_ref[...]
        ms = jnp.mean(x * x, axis=-1, keepdims=True)
        hn_ref[...] = (x * lax.rsqrt(ms + EPS) * gn_ref[...]).astype(hn_ref.dtype)


def _ffn(h, x, w13, w2, g_next, tm=512, fc=1408):
    t, d = x.shape
    n_chunks = D_FF // fc
    return pl.pallas_call(
        _ffn_kernel,
        out_shape=(jax.ShapeDtypeStruct((t, d), F32), jax.ShapeDtypeStruct((t, d), BF16)),
        grid=(t // tm, n_chunks),
        in_specs=[pl.BlockSpec((tm, d), lambda i, c: (i, 0)),
                  pl.BlockSpec((tm, d), lambda i, c: (i, 0)),
                  pl.BlockSpec((d, fc), lambda i, c: (0, c)),
                  pl.BlockSpec((d, fc), lambda i, c: (0, c + n_chunks)),
                  pl.BlockSpec((fc, d), lambda i, c: (c, 0)),
                  pl.BlockSpec((1, d), lambda i, c: (0, 0))],
        out_specs=(pl.BlockSpec((tm, d), lambda i, c: (i, 0)), pl.BlockSpec((tm, d), lambda i, c: (i, 0))),
        compiler_params=_cparams("parallel", "arbitrary"),
        name="ffn",
    )(h, x, w13, w13, w2, g_next)


def _rope_tables(s):
    pos = jnp.arange(s, dtype=F32)
    inv = 1.0 / (ROPE_THETA ** (jnp.arange(0, HEAD_DIM, 2, dtype=F32) / HEAD_DIM))
    ang = pos[:, None] * inv[None, :]
    cos, sin = jnp.cos(ang), jnp.sin(ang)
    cos128 = jnp.concatenate([cos, cos, cos, cos], axis=-1)
    sin128 = jnp.concatenate([-sin, sin, -sin, sin], axis=-1)
    return cos128, sin128


def kernel(x, norm1_g, w_in, a_conv_w, a_w_out, b_conv_w, b_conv_b, b_ln_g, b_ln_b, b_w_out,
           q_norm_g, k_norm_g, cmp_pe, cmp_w1, cmp_b1, cmp_w2, nsa_w_out, w_o, norm2_g,
           ffn_w13, ffn_w2):
    b, s, d = x.shape
    depth = w_in.shape[0]
    t = b * s
    cos128, sin128 = _rope_tables(s)
    hid = jnp.arange(128) // HEAD_DIM
    mblk = jnp.where(hid[:, None] == hid[None, :], 1.0 / HEAD_DIM, 0.0).astype(BF16)

    w_in_p = jnp.concatenate(
        [w_in[:, :, :GATE_COL + N_GATE],
         jnp.zeros((depth, d, GATE_PAD - N_GATE), w_in.dtype),
         w_in[:, :, GATE_COL + N_GATE:]], axis=-1).astype(BF16)
    a_w_out_b = a_w_out.astype(BF16)
    b_w_out_b = b_w_out.astype(BF16)
    cmp_w1_b = cmp_w1.astype(BF16)
    cmp_w2_b = cmp_w2.astype(BF16)
    nsa_w_out_b = nsa_w_out.astype(BF16)
    w_o_b = w_o.astype(BF16)
    ffn_w13_b = ffn_w13.astype(BF16)
    ffn_w2_b = ffn_w2.astype(BF16)
    qg128 = jnp.concatenate([q_norm_g, q_norm_g], axis=-1)[:, None, :]
    kg128 = jnp.concatenate([k_norm_g, k_norm_g], axis=-1)
    pe2 = cmp_pe.reshape(depth, 2, 2, (CMP_BLOCK // 2) * HEAD_DIM)
    b1 = cmp_b1[:, :, None, :]

    xf = x.reshape(t, d)
    h = _rmsnorm(xf, norm1_g[0:1])
    for l in range(depth):
        proj = _matmul(h, w_in_p[l], BF16)
        proj3 = proj.reshape(b, s, PROJ_WIDTH)
        mab = _conv_mixers(proj3, a_conv_w[l], b_conv_w[l], b_conv_b[l:l + 1], b_ln_g[l:l + 1],
                           b_ln_b[l:l + 1], a_w_out_b[l], b_w_out_b[l])
        qh, kh, vh, gates = _attn_prep(proj3, cos128, sin128, qg128[l], kg128[l], mblk)
        nb = s // CMP_STRIDE
        kh5 = kh.reshape(b, 3, N_KV, nb, CMP_STRIDE * HEAD_DIM)
        vh5 = vh.reshape(b, 3, N_KV, nb, CMP_STRIDE * HEAD_DIM)
        kc, vc = _compress(kh5, vh5, pe2[l], cmp_w1_b[l], b1[l], cmp_w2_b[l])
        o = _attention(qh, kc, vc, kh, vh, gates)
        xf, h2 = _merge(o.reshape(t, ATTN_WIDTH), mab.reshape(t, d), proj, xf,
                        nsa_w_out_b[l], w_o_b[l], norm2_g[l:l + 1])
        g_next = norm1_g[l + 1:l + 2] if l + 1 < depth else norm1_g[l:l + 1]
        xf, h = _ffn(h2, xf, ffn_w13_b[l], ffn_w2_b[l], g_next)
    return xf.reshape(b, s, d)
```

```python
import functools

import jax
import jax.numpy as jnp
from jax import lax
from jax.experimental import pallas as pl
from jax.experimental.pallas import tpu as pltpu

F32 = jnp.float32
BF16 = jnp.bfloat16

D_MODEL = 1024
DEPTH = 4
CONV_A_WIDTH = 512
CONV_A_K = 3
CONV_B_WIDTH = 512
CONV_B_K = 31
N_HEADS = 8
N_KV = 2
HEAD_DIM = 64
GROUP = N_HEADS // N_KV
ATTN_WIDTH = N_HEADS * HEAD_DIM
KV_WIDTH = N_KV * HEAD_DIM
N_BRANCH = 3
CMP_BLOCK = 32
CMP_STRIDE = 16
CMP_HIDDEN = 256
SLC_BLOCK = 64
SLC_TOPN = 16
WINDOW = 512
ROPE_THETA = 10000.0
D_FF = -(-8 * D_MODEL // (3 * 256)) * 256
EPS = 1e-6
NEG = -1e30
SCALE = HEAD_DIM ** -0.5
QSCALE = SCALE * 1.4426950408889634

N_GATE = N_HEADS * N_BRANCH
GATE_COL = 3 * CONV_A_WIDTH + 2 * CONV_B_WIDTH + ATTN_WIDTH + 6 * KV_WIDTH
GATE_PAD = 256
PROJ_WIDTH = GATE_COL + GATE_PAD + 3 * D_MODEL

VMEM_LIMIT = 48 * 1024 * 1024
HALO = 32
CONV_ROWS = 32
ATTN_CHUNK = 256
GATE_ROWS = 16


def _cparams(*sem):
    return pltpu.CompilerParams(dimension_semantics=sem, vmem_limit_bytes=VMEM_LIMIT)


def _rmsnorm_kernel(x_ref, g_ref, h_ref):
    x = x_ref[...]
    ms = jnp.mean(x * x, axis=-1, keepdims=True)
    h_ref[...] = (x * lax.rsqrt(ms + EPS) * g_ref[...]).astype(h_ref.dtype)


def _rmsnorm(x, g, tm=1024):
    t, d = x.shape
    return pl.pallas_call(
        _rmsnorm_kernel,
        out_shape=jax.ShapeDtypeStruct((t, d), BF16),
        grid=(t // tm,),
        in_specs=[pl.BlockSpec((tm, d), lambda i: (i, 0)), pl.BlockSpec((1, d), lambda i: (0, 0))],
        out_specs=pl.BlockSpec((tm, d), lambda i: (i, 0)),
        compiler_params=_cparams("parallel"),
        name="rmsnorm",
    )(x, g)


def _mm_kernel(a_ref, w_ref, o_ref):
    o_ref[...] = jnp.dot(a_ref[...], w_ref[...], preferred_element_type=F32).astype(o_ref.dtype)


def _matmul(a, w, out_dtype, tm=1024, tn=1024):
    m, k = a.shape
    n = w.shape[1]
    return pl.pallas_call(
        _mm_kernel,
        out_shape=jax.ShapeDtypeStruct((m, n), out_dtype),
        grid=(m // tm, n // tn),
        in_specs=[pl.BlockSpec((tm, k), lambda i, j: (i, 0)), pl.BlockSpec((k, tn), lambda i, j: (0, j))],
        out_specs=pl.BlockSpec((tm, tn), lambda i, j: (i, j)),
        compiler_params=_cparams("parallel", "parallel"),
        name="in_proj",
    )(a, w)


def _conv_kernel(ac_ref, ab_ref, ah_ref, ba_ref, bg_ref, ga_ref, gb_ref,
                 wa_ref, wb_ref, bb_ref, lng_ref, lnb_ref, wao_ref, wbo_ref,
                 out_ref, ua_ext, ub_ext, vb_scr):
    ts = out_ref.shape[1]
    width = ua_ext.shape[1]

    @pl.when(pl.program_id(1) == 0)
    def _():
        ua_ext[0:HALO, :] = jnp.zeros((HALO, width), F32)
        ub_ext[0:HALO, :] = jnp.zeros((HALO, width), F32)

    ua_ext[HALO:HALO + ts, :] = ac_ref[0].astype(F32) * ah_ref[0].astype(F32)
    ub_ext[HALO:HALO + ts, :] = ba_ref[0].astype(F32) * jax.nn.sigmoid(bg_ref[0].astype(F32))

    wa = wa_ref[...]
    va = wa[0:1] * ua_ext[HALO - 2:HALO - 2 + ts, :]
    va = va + wa[1:2] * ua_ext[HALO - 1:HALO - 1 + ts, :]
    va = va + wa[2:3] * ua_ext[HALO:HALO + ts, :]
    ya = jnp.dot((ab_ref[0].astype(F32) * va).astype(BF16), wao_ref[...], preferred_element_type=F32)

    wb = wb_ref[...]
    bias = jnp.broadcast_to(bb_ref[...], (CONV_ROWS, width))
    for c in range(ts // CONV_ROWS):
        acc = bias
        for k in range(CONV_B_K):
            r0 = HALO - (CONV_B_K - 1) + k + c * CONV_ROWS
            acc = acc + wb[k:k + 1] * ub_ext[r0:r0 + CONV_ROWS, :]
        vb_scr[c * CONV_ROWS:(c + 1) * CONV_ROWS, :] = acc
    u = vb_scr[...]
    mu = jnp.mean(u, axis=-1, keepdims=True)
    var = jnp.mean(jnp.square(u - mu), axis=-1, keepdims=True)
    y = (u - mu) * lax.rsqrt(var + EPS)
    y = y * lng_ref[...] + lnb_ref[...]
    y = y * jax.nn.sigmoid(y)
    yb = jnp.dot(y.astype(BF16), wbo_ref[...], preferred_element_type=F32)

    out_ref[0] = (jax.nn.sigmoid(ga_ref[0].astype(F32)) * ya
                  + jax.nn.sigmoid(gb_ref[0].astype(F32)) * yb)

    ua_ext[0:HALO, :] = ua_ext[ts:ts + HALO, :]
    ub_ext[0:HALO, :] = ub_ext[ts:ts + HALO, :]


def _conv_mixers(proj, wa, wb, bb, lng, lnb, wao, wbo, ts=512):
    b, s, _ = proj.shape
    w = CONV_A_WIDTH
    gate_blk = (GATE_COL + GATE_PAD) // D_MODEL

    def col(width, idx):
        return pl.BlockSpec((1, ts, width), lambda bi, si: (bi, si, idx))

    def full(arr):
        return pl.BlockSpec(arr.shape, lambda bi, si: (0,) * arr.ndim)

    return pl.pallas_call(
        _conv_kernel,
        out_shape=jax.ShapeDtypeStruct((b, s, D_MODEL), F32),
        grid=(b, s // ts),
        in_specs=[col(w, 0), col(w, 1), col(w, 2), col(w, 3), col(w, 4),
                  col(D_MODEL, gate_blk), col(D_MODEL, gate_blk + 1),
                  full(wa), full(wb), full(bb), full(lng), full(lnb), full(wao), full(wbo)],
        out_specs=pl.BlockSpec((1, ts, D_MODEL), lambda bi, si: (bi, si, 0)),
        scratch_shapes=[pltpu.VMEM((HALO + ts, w), F32), pltpu.VMEM((HALO + ts, w), F32),
                        pltpu.VMEM((ts, w), F32)],
        compiler_params=_cparams("parallel", "arbitrary"),
        name="conv_mixers",
    )(proj, proj, proj, proj, proj, proj, proj, wa, wb, bb, lng, lnb, wao, wbo)


def _prep_kernel(q_ref, kv_ref, gt_ref, cos_ref, sin_ref, qg_ref, kg_ref, mblk_ref,
                 qt_ref, kcmp_ref, vcmp_ref, ka_ref, vt_ref, go_ref):
    ts = q_ref.shape[1]
    ck = vt_ref.shape[5]
    cos = cos_ref[...]
    sin = sin_ref[...]
    mblk = mblk_ref[...]
    lane = lax.broadcasted_iota(jnp.int32, (ts, 128), 1)
    first_half = (lane & (HEAD_DIM - 1)) < HEAD_DIM // 2
    low = lane < HEAD_DIM
    tabs = pl.program_id(1) * ts + lax.broadcasted_iota(jnp.int32, (ts, 128), 0)
    blk_onehot = jnp.where(lane - HEAD_DIM == tabs // SLC_BLOCK, 1.0, 0.0)
    ones_rows = jnp.where(lax.broadcasted_iota(jnp.int32, (128 - HEAD_DIM, ck), 0) == 0, 1.0, 0.0).astype(BF16)

    def per_group(x):
        return x, pltpu.roll(x, HEAD_DIM, 1)

    def norm_rope(x, g):
        xx = x * x
        hi = xx.astype(BF16)
        lo = (xx - hi.astype(F32)).astype(BF16)
        ms = (jnp.dot(hi, mblk, preferred_element_type=F32)
              + jnp.dot(lo, mblk, preferred_element_type=F32))
        y = x * lax.rsqrt(ms + EPS) * g
        partner = jnp.where(first_half, pltpu.roll(y, 128 - HEAD_DIM // 2, 1), pltpu.roll(y, HEAD_DIM // 2, 1))
        return y * cos + partner * sin

    qg = qg_ref[...]
    for j in range(ATTN_WIDTH // 128):
        o_t = (norm_rope(q_ref[0, :, 128 * j:128 * (j + 1)].astype(F32), qg) * QSCALE).T
        qt_ref[0, 2 * j] = o_t[:HEAD_DIM].astype(BF16)
        qt_ref[0, 2 * j + 1] = o_t[HEAD_DIM:].astype(BF16)
    for i in range(3):
        k = norm_rope(kv_ref[0, :, 256 * i:256 * i + 128].astype(F32), kg_ref[i:i + 1, :])
        v = kv_ref[0, :, 256 * i + 128:256 * i + 256].astype(F32)
        if i == 0:
            for g, (kg, vg) in enumerate(zip(per_group(k), per_group(v))):
                kcmp_ref[0, g] = kg[:, :HEAD_DIM].astype(BF16)
                vcmp_ref[0, g] = vg[:, :HEAD_DIM].astype(BF16)
        else:
            extra = blk_onehot if i == 1 else 0.0
            for g, kg in enumerate(per_group(k)):
                ka_ref[0, i - 1, g] = jnp.where(low, kg, extra).astype(BF16)
            v_t = v.T
            for g in range(N_KV):
                for c in range(ts // ck):
                    vt_ref[0, i - 1, g, c, :HEAD_DIM, :] = (
                        v_t[g * HEAD_DIM:(g + 1) * HEAD_DIM, c * ck:(c + 1) * ck].astype(BF16))
                    vt_ref[0, i - 1, g, c, HEAD_DIM:, :] = ones_rows
    sg = jax.nn.sigmoid(gt_ref[0].astype(F32))
    ngate = go_ref.shape[2]
    go_ref[0, 0] = sg.T[:ngate]
    go_ref[0, 1] = pltpu.roll(sg, 128 - GROUP * N_BRANCH, 1).T[:ngate]


def _attn_prep(proj, cos128, sin128, qg128, kg128, mblk, ts=512, ck=ATTN_CHUNK):
    b, s, _ = proj.shape

    def full(arr):
        return pl.BlockSpec(arr.shape, lambda bi, si: (0,) * arr.ndim)

    q_blk = (3 * CONV_A_WIDTH + 2 * CONV_B_WIDTH) // ATTN_WIDTH
    kv_blk = (q_blk + 1) * ATTN_WIDTH // (6 * KV_WIDTH)
    gt_blk = GATE_COL // 128
    return pl.pallas_call(
        _prep_kernel,
        out_shape=(jax.ShapeDtypeStruct((b, N_HEADS, HEAD_DIM, s), BF16),
                   jax.ShapeDtypeStruct((b, N_KV, s, HEAD_DIM), BF16),
                   jax.ShapeDtypeStruct((b, N_KV, s, HEAD_DIM), BF16),
                   jax.ShapeDtypeStruct((b, 2, N_KV, s, 128), BF16),
                   jax.ShapeDtypeStruct((b, 2, N_KV, s // ck, 128, ck), BF16),
                   jax.ShapeDtypeStruct((b, N_KV, GATE_ROWS, s), F32)),
        grid=(b, s // ts),
        in_specs=[pl.BlockSpec((1, ts, ATTN_WIDTH), lambda bi, si: (bi, si, q_blk)),
                  pl.BlockSpec((1, ts, 6 * KV_WIDTH), lambda bi, si: (bi, si, kv_blk)),
                  pl.BlockSpec((1, ts, 128), lambda bi, si: (bi, si, gt_blk)),
                  pl.BlockSpec((ts, 128), lambda bi, si: (si, 0)),
                  pl.BlockSpec((ts, 128), lambda bi, si: (si, 0)),
                  full(qg128), full(kg128), full(mblk)],
        out_specs=(pl.BlockSpec((1, N_HEADS, HEAD_DIM, ts), lambda bi, si: (bi, 0, 0, si)),
                   pl.BlockSpec((1, N_KV, ts, HEAD_DIM), lambda bi, si: (bi, 0, si, 0)),
                   pl.BlockSpec((1, N_KV, ts, HEAD_DIM), lambda bi, si: (bi, 0, si, 0)),
                   pl.BlockSpec((1, 2, N_KV, ts, 128), lambda bi, si: (bi, 0, 0, si, 0)),
                   pl.BlockSpec((1, 2, N_KV, ts // ck, 128, ck), lambda bi, si: (bi, 0, 0, si, 0, 0)),
                   pl.BlockSpec((1, N_KV, GATE_ROWS, ts), lambda bi, si: (bi, 0, 0, si))),
        compiler_params=_cparams("parallel", "parallel"),
        name="attn_prep",
    )(proj, proj, proj, cos128, sin128, qg128, kg128, mblk)


def _compress_kernel(k_ref, v_ref, pe_ref, w1_ref, b1_ref, w2_ref, kc_ref, vc_ref):
    nb = k_ref.shape[2]
    half = w1_ref.shape[1] // 2
    row = lax.broadcasted_iota(jnp.int32, (nb, w2_ref.shape[2]), 0)
    for i, (src, dst) in enumerate(((k_ref, kc_ref), (v_ref, vc_ref))):
        u = src[0, 0].astype(F32)
        top = (u + pe_ref[i, 0:1, :]).astype(BF16)
        bot = (u + pe_ref[i, 1:2, :]).astype(BF16)
        a = jnp.dot(top, w1_ref[i, :half, :], preferred_element_type=F32)
        bm = jnp.dot(bot, w1_ref[i, half:, :], preferred_element_type=F32)
        hid = a + pltpu.roll(bm, nb - 1, 0) + b1_ref[i]
        hid = jax.nn.gelu(hid)
        out = jnp.dot(hid.astype(BF16), w2_ref[i], preferred_element_type=F32)
        out = jnp.where(row < nb - 1, out, 0.0)
        dst[0, 0] = (out if i == 0 else out.T).astype(dst.dtype)


def _compress(k4, v4, pe, w1, b1, w2):
    b, _, nb, _ = k4.shape
    width = w2.shape[2]
    assert nb == width

    def full(arr):
        return pl.BlockSpec(arr.shape, lambda bi, gi: (0,) * arr.ndim)

    blk = pl.BlockSpec((1, 1, nb, CMP_STRIDE * HEAD_DIM), lambda bi, gi: (bi, gi, 0, 0))
    out_blk = pl.BlockSpec((1, 1, nb, width), lambda bi, gi: (bi, gi, 0, 0))
    return pl.pallas_call(
        _compress_kernel,
        out_shape=(jax.ShapeDtypeStruct((b, N_KV, nb, width), BF16),) * 2,
        grid=(b, N_KV),
        in_specs=[blk, blk, full(pe), full(w1), full(b1), full(w2)],
        out_specs=(out_blk, out_blk),
        compiler_params=_cparams("parallel", "parallel"),
        name="compress",
    )(k4, v4, pe, w1, b1, w2)


def _attn_kernel(q_ref, kc_ref, vct_ref, ks_ref, vs_ref, kw_ref, vw_ref, gt_ref, o_ref,
                 qa_scr, m_scr, acc_scr):
    tq = q_ref.shape[3]
    ck = vs_ref.shape[5]
    s_len = ks_ref.shape[3]
    nc = kc_ref.shape[2]
    n_sel = s_len // SLC_BLOCK
    cols = GROUP * tq
    qi = pl.program_id(2)
    t0 = qi * tq

    def head(r):
        return slice(r * tq, (r + 1) * tq)

    for r in range(GROUP):
        qa_scr[:HEAD_DIM, head(r)] = q_ref[0, r]
    qa_scr[HEAD_DIM:, :] = jnp.zeros((128 - HEAD_DIM, cols), BF16)

    tpos = t0 + (lax.broadcasted_iota(jnp.int32, (nc, cols), 1) & (tq - 1))
    cidx = lax.broadcasted_iota(jnp.int32, (nc, cols), 0)
    cmask = ((cidx * CMP_STRIDE + (CMP_BLOCK - 1)) <= tpos) & (cidx < nc - 1)
    sc = jnp.dot(kc_ref[0, 0], qa_scr[...], preferred_element_type=F32)
    sc = jnp.where(cmask, sc, NEG)
    mc = jnp.max(sc, axis=0, keepdims=True)
    ec = jnp.where(cmask, jnp.exp2(sc - mc), 0.0)
    dc = jnp.sum(ec, axis=0, keepdims=True)
    pc = ec / jnp.where(dc > 0.0, dc, 1.0)
    o_cmp = jnp.dot(vct_ref[0, 0], pc.astype(BF16), preferred_element_type=F32)

    psum = pc[:, head(0)] + pc[:, head(1)] + pc[:, head(2)] + pc[:, head(3)]
    p_hi = psum.astype(BF16)
    p_lo = (psum - p_hi.astype(F32)).astype(BF16)
    jrow = lax.broadcasted_iota(jnp.int32, (n_sel, nc), 0)
    ccol = lax.broadcasted_iota(jnp.int32, (n_sel, nc), 1)
    ovl = ((ccol * CMP_STRIDE < jrow * SLC_BLOCK + SLC_BLOCK)
           & (ccol * CMP_STRIDE + CMP_BLOCK > jrow * SLC_BLOCK))
    ovl = jnp.where(ovl, 1.0, 0.0).astype(BF16)
    imp = (jnp.dot(ovl, p_hi, preferred_element_type=F32)
           + jnp.dot(ovl, p_lo, preferred_element_type=F32))

    jj = lax.broadcasted_iota(jnp.int32, (n_sel, tq), 0)
    cur = (t0 + lax.broadcasted_iota(jnp.int32, (n_sel, tq), 1)) // SLC_BLOCK
    valid = jj <= cur
    forced = (jj == 0) | (jj == cur) | (jj == cur - 1)
    score = jnp.where(forced, jnp.inf, jnp.where(valid, imp, -jnp.inf))
    rank = jnp.zeros((n_sel, tq), F32)
    for jp in range(n_sel):
        other = score[jp:jp + 1, :]
        ahead = (other > score) | ((other == score) & (jj > jp))
        rank = rank + jnp.where(ahead, 1.0, 0.0)
    sel = jnp.where(valid & (rank < float(min(SLC_TOPN, n_sel))), 1.0, 0.0)

    pen = jnp.where(sel > 0.5, 0.0, NEG).astype(BF16)
    for r in range(GROUP):
        qa_scr[HEAD_DIM:HEAD_DIM + n_sel, head(r)] = pen

    kj = lax.broadcasted_iota(jnp.int32, (ck, tq), 0)
    qc = lax.broadcasted_iota(jnp.int32, (ck, tq), 1)
    causal = jnp.where(kj <= qc, 0.0, NEG)
    tail = jnp.where(kj > qc, 0.0, NEG)

    def attend(k_ref, v_ref, chunk, mask):
        k = k_ref[0, 0, 0, pl.ds(pl.multiple_of(chunk * ck, ck), ck), :]
        s = jnp.dot(k, qa_scr[...], preferred_element_type=F32)
        if mask is not None:
            s = s + jnp.tile(mask, (1, GROUP))
        m_prev = m_scr[...]
        m_new = jnp.maximum(m_prev, jnp.max(s, axis=0, keepdims=True))
        alpha = jnp.exp2(m_prev - m_new)
        p = jnp.exp2(s - m_new).astype(BF16)
        acc_scr[...] = alpha * acc_scr[...] + jnp.dot(v_ref[0, 0, 0, chunk], p, preferred_element_type=F32)
        m_scr[...] = m_new

    def reset():
        m_scr[...] = jnp.full(m_scr.shape, NEG, F32)
        acc_scr[...] = jnp.zeros(acc_scr.shape, F32)

    def finish():
        acc = acc_scr[...]
        return acc[:HEAD_DIM] / acc[HEAD_DIM:HEAD_DIM + 1]

    reset()

    def slc_body(c, carry):
        attend(ks_ref, vs_ref, c, None)
        return carry

    lax.fori_loop(0, qi, slc_body, 0)
    attend(ks_ref, vs_ref, qi, causal)
    o_slc = finish()

    reset()
    attend(kw_ref, vw_ref, qi, causal)

    @pl.when(qi >= 1)
    def _():
        attend(kw_ref, vw_ref, qi - 1, None)

    @pl.when(qi >= WINDOW // ck)
    def _():
        attend(kw_ref, vw_ref, qi - WINDOW // ck, tail)

    o_win = finish()

    gt = gt_ref[0, 0]
    outs = []
    for r in range(GROUP):
        g0 = gt[N_BRANCH * r:N_BRANCH * r + 1]
        g1 = gt[N_BRANCH * r + 1:N_BRANCH * r + 2]
        g2 = gt[N_BRANCH * r + 2:N_BRANCH * r + 3]
        outs.append(g0 * o_cmp[:HEAD_DIM, head(r)] + g1 * o_slc[:, head(r)] + g2 * o_win[:, head(r)])
    o_ref[0] = jnp.concatenate(outs, axis=0).T.astype(o_ref.dtype)


def _attention(qt, kc, vct, ka, vt, gates_t, tq=ATTN_CHUNK):
    b, _, _, s = qt.shape
    nc = kc.shape[2]
    ck = vt.shape[5]
    cols = GROUP * tq
    assert tq == ck and WINDOW % ck == 0 and tq & (tq - 1) == 0

    kseq = lambda branch: pl.BlockSpec((1, 1, 1, s, 128), lambda bi, gi, qi: (bi, branch, gi, 0, 0))
    vseq = lambda branch: pl.BlockSpec((1, 1, 1, s // ck, 128, ck), lambda bi, gi, qi: (bi, branch, gi, 0, 0, 0))
    cblk = pl.BlockSpec((1, 1, nc, 128), lambda bi, gi, qi: (bi, gi, 0, 0))
    return pl.pallas_call(
        _attn_kernel,
        out_shape=jax.ShapeDtypeStruct((b, s, ATTN_WIDTH), BF16),
        grid=(b, N_KV, s // tq),
        in_specs=[pl.BlockSpec((1, GROUP, HEAD_DIM, tq), lambda bi, gi, qi: (bi, gi, 0, qi)),
                  cblk, cblk, kseq(0), vseq(0), kseq(1), vseq(1),
                  pl.BlockSpec((1, 1, GATE_ROWS, tq), lambda bi, gi, qi: (bi, gi, 0, qi))],
        out_specs=pl.BlockSpec((1, tq, GROUP * HEAD_DIM), lambda bi, gi, qi: (bi, qi, gi)),
        scratch_shapes=[pltpu.VMEM((128, cols), BF16),
                        pltpu.VMEM((1, cols), F32),
                        pltpu.VMEM((128, cols), F32)],
        compiler_params=_cparams("parallel", "parallel", "arbitrary"),
        name="nsa_attention",
    )(qt, kc, vct, ka, vt, ka, vt, gates_t)


def _merge_kernel(o_ref, mab_ref, gc_ref, x_ref, wn_ref, wo_ref, g2_ref, xo_ref, h_ref):
    yc = jnp.dot(o_ref[...], wn_ref[...], preferred_element_type=F32)
    mixed = mab_ref[...] + jax.nn.sigmoid(gc_ref[...].astype(F32)) * yc
    x = x_ref[...] + jnp.dot(mixed.astype(BF16), wo_ref[...], preferred_element_type=F32)
    xo_ref[...] = x
    ms = jnp.mean(x * x, axis=-1, keepdims=True)
    h_ref[...] = (x * lax.rsqrt(ms + EPS) * g2_ref[...]).astype(h_ref.dtype)


def _merge(o, mab, proj2d, x, wn, wo, g2, tm=512):
    t, d = x.shape
    gc_blk = (GATE_COL + GATE_PAD) // D_MODEL + 2

    def full(arr):
        return pl.BlockSpec(arr.shape, lambda i: (0,) * arr.ndim)

    return pl.pallas_call(
        _merge_kernel,
        out_shape=(jax.ShapeDtypeStruct((t, d), F32), jax.ShapeDtypeStruct((t, d), BF16)),
        grid=(t // tm,),
        in_specs=[pl.BlockSpec((tm, ATTN_WIDTH), lambda i: (i, 0)),
                  pl.BlockSpec((tm, d), lambda i: (i, 0)),
                  pl.BlockSpec((tm, d), lambda i: (i, gc_blk)),
                  pl.BlockSpec((tm, d), lambda i: (i, 0)),
                  full(wn), full(wo), full(g2)],
        out_specs=(pl.BlockSpec((tm, d), lambda i: (i, 0)), pl.BlockSpec((tm, d), lambda i: (i, 0))),
        compiler_params=_cparams("parallel"),
        name="merge_out_proj",
    )(o, mab, proj2d, x, wn, wo, g2)


def _ffn_kernel(h_ref, x_ref, w1_ref, w3_ref, w2_ref, gn_ref, xo_ref, hn_ref):
    c = pl.program_id(1)

    @pl.when(c == 0)
    def _():
        xo_ref[...] = x_ref[...]

    h = h_ref[...]
    u = jnp.dot(h, w1_ref[...], preferred_element_type=F32)
    v = jnp.dot(h, w3_ref[...], preferred_element_type=F32)
    act = (u * jax.nn.sigmoid(u) * v).astype(BF16)
    xo_ref[...] += jnp.dot(act, w2_ref[...], preferred_element_type=F32)

    @pl.when(c == pl.num_programs(1) - 1)
    def _():
        x = xo_ref[...]
        ms = jnp.mean(x * x, axis=-1, keepdims=True)
        hn_ref[...] = (x * lax.rsqrt(ms + EPS) * gn_ref[...]).astype(hn_ref.dtype)


def _ffn(h, x, w13, w2, g_next, tm=512, fc=1408):
    t, d = x.shape
    n_chunks = D_FF // fc
    return pl.pallas_call(
        _ffn_kernel,
        out_shape=(jax.ShapeDtypeStruct((t, d), F32), jax.ShapeDtypeStruct((t, d), BF16)),
        grid=(t // tm, n_chunks),
        in_specs=[pl.BlockSpec((tm, d), lambda i, c: (i, 0)),
                  pl.BlockSpec((tm, d), lambda i, c: (i, 0)),
                  pl.BlockSpec((d, fc), lambda i, c: (0, c)),
                  pl.BlockSpec((d, fc), lambda i, c: (0, c + n_chunks)),
                  pl.BlockSpec((fc, d), lambda i, c: (c, 0)),
                  pl.BlockSpec((1, d), lambda i, c: (0, 0))],
        out_specs=(pl.BlockSpec((tm, d), lambda i, c: (i, 0)), pl.BlockSpec((tm, d), lambda i, c: (i, 0))),
        compiler_params=_cparams("parallel", "arbitrary"),
        name="ffn",
    )(h, x, w13, w13, w2, g_next)


def _rope_tables(s):
    pos = jnp.arange(s, dtype=F32)
    inv = 1.0 / (ROPE_THETA ** (jnp.arange(0, HEAD_DIM, 2, dtype=F32) / HEAD_DIM))
    ang = pos[:, None] * inv[None, :]
    cos, sin = jnp.cos(ang), jnp.sin(ang)
    cos128 = jnp.concatenate([cos, cos, cos, cos], axis=-1)
    sin128 = jnp.concatenate([-sin, sin, -sin, sin], axis=-1)
    return cos128, sin128


def kernel(x, norm1_g, w_in, a_conv_w, a_w_out, b_conv_w, b_conv_b, b_ln_g, b_ln_b, b_w_out,
           q_norm_g, k_norm_g, cmp_pe, cmp_w1, cmp_b1, cmp_w2, nsa_w_out, w_o, norm2_g,
           ffn_w13, ffn_w2):
    b, s, d = x.shape
    depth = w_in.shape[0]
    t = b * s
    cos128, sin128 = _rope_tables(s)
    hid = jnp.arange(128) // HEAD_DIM
    mblk = jnp.where(hid[:, None] == hid[None, :], 1.0 / HEAD_DIM, 0.0).astype(BF16)

    w_in_p = jnp.concatenate(
        [w_in[:, :, :GATE_COL + N_GATE],
         jnp.zeros((depth, d, GATE_PAD - N_GATE), w_in.dtype),
         w_in[:, :, GATE_COL + N_GATE:]], axis=-1).astype(BF16)
    a_w_out_b = a_w_out.astype(BF16)
    b_w_out_b = b_w_out.astype(BF16)
    cmp_w1_b = cmp_w1.astype(BF16)
    cmp_w2_b = jnp.pad(cmp_w2, ((0, 0), (0, 0), (0, 0), (0, 128 - HEAD_DIM))).astype(BF16)
    nsa_w_out_b = nsa_w_out.astype(BF16)
    w_o_b = w_o.astype(BF16)
    ffn_w13_b = ffn_w13.astype(BF16)
    ffn_w2_b = ffn_w2.astype(BF16)
    qg128 = jnp.concatenate([q_norm_g, q_norm_g], axis=-1)[:, None, :]
    kg128 = jnp.concatenate([k_norm_g, k_norm_g], axis=-1)
    pe2 = cmp_pe.reshape(depth, 2, 2, (CMP_BLOCK // 2) * HEAD_DIM)
    b1 = cmp_b1[:, :, None, :]

    xf = x.reshape(t, d)
    h = _rmsnorm(xf, norm1_g[0:1])
    for l in range(depth):
        proj = _matmul(h, w_in_p[l], BF16)
        proj3 = proj.reshape(b, s, PROJ_WIDTH)
        mab = _conv_mixers(proj3, a_conv_w[l], b_conv_w[l], b_conv_b[l:l + 1], b_ln_g[l:l + 1],
                           b_ln_b[l:l + 1], a_w_out_b[l], b_w_out_b[l])
        qt, kcmp, vcmp, ka, vt, gates_t = _attn_prep(proj3, cos128, sin128, qg128[l], kg128[l], mblk)
        nb = s // CMP_STRIDE
        kc, vct = _compress(kcmp.reshape(b, N_KV, nb, CMP_STRIDE * HEAD_DIM),
                            vcmp.reshape(b, N_KV, nb, CMP_STRIDE * HEAD_DIM),
                            pe2[l], cmp_w1_b[l], b1[l], cmp_w2_b[l])
        o = _attention(qt, kc, vct, ka, vt, gates_t)
        xf, h2 = _merge(o.reshape(t, ATTN_WIDTH), mab.reshape(t, d), proj, xf,
                        nsa_w_out_b[l], w_o_b[l], norm2_g[l:l + 1])
        g_next = norm1_g[l + 1:l + 2] if l + 1 < depth else norm1_g[l:l + 1]
        xf, h = _ffn(h2, xf, ffn_w13_b[l], ffn_w2_b[l], g_next)
    return xf.reshape(b, s, d)
```

```python
import functools

import jax
import jax.numpy as jnp
from jax import lax
from jax.experimental import pallas as pl
from jax.experimental.pallas import tpu as pltpu

F32 = jnp.float32
BF16 = jnp.bfloat16

D_MODEL = 1024
DEPTH = 4
CONV_A_WIDTH = 512
CONV_A_K = 3
CONV_B_WIDTH = 512
CONV_B_K = 31
N_HEADS = 8
N_KV = 2
HEAD_DIM = 64
GROUP = N_HEADS // N_KV
ATTN_WIDTH = N_HEADS * HEAD_DIM
KV_WIDTH = N_KV * HEAD_DIM
N_BRANCH = 3
CMP_BLOCK = 32
CMP_STRIDE = 16
CMP_HIDDEN = 256
SLC_BLOCK = 64
SLC_TOPN = 16
WINDOW = 512
ROPE_THETA = 10000.0
D_FF = -(-8 * D_MODEL // (3 * 256)) * 256
EPS = 1e-6
NEG = -1e30
SCALE = HEAD_DIM ** -0.5
QSCALE = SCALE * 1.4426950408889634

N_GATE = N_HEADS * N_BRANCH
GATE_COL = 3 * CONV_A_WIDTH + 2 * CONV_B_WIDTH + ATTN_WIDTH + 6 * KV_WIDTH
GATE_PAD = 256
PROJ_WIDTH = GATE_COL + GATE_PAD + 3 * D_MODEL

VMEM_LIMIT = 48 * 1024 * 1024
HALO = 32
CONV_ROWS = 32
ATTN_CHUNK = 256
GATE_ROWS = 16
VT_ROWS = HEAD_DIM + 16


def _cparams(*sem):
    return pltpu.CompilerParams(dimension_semantics=sem, vmem_limit_bytes=VMEM_LIMIT)


def _rmsnorm_kernel(x_ref, g_ref, h_ref):
    x = x_ref[...]
    ms = jnp.mean(x * x, axis=-1, keepdims=True)
    h_ref[...] = (x * lax.rsqrt(ms + EPS) * g_ref[...]).astype(h_ref.dtype)


def _rmsnorm(x, g, tm=1024):
    t, d = x.shape
    return pl.pallas_call(
        _rmsnorm_kernel,
        out_shape=jax.ShapeDtypeStruct((t, d), BF16),
        grid=(t // tm,),
        in_specs=[pl.BlockSpec((tm, d), lambda i: (i, 0)), pl.BlockSpec((1, d), lambda i: (0, 0))],
        out_specs=pl.BlockSpec((tm, d), lambda i: (i, 0)),
        compiler_params=_cparams("parallel"),
        name="rmsnorm",
    )(x, g)


def _mm_kernel(a_ref, w_ref, o_ref):
    o_ref[...] = jnp.dot(a_ref[...], w_ref[...], preferred_element_type=F32).astype(o_ref.dtype)


def _matmul(a, w, out_dtype, tm=1024, tn=1024):
    m, k = a.shape
    n = w.shape[1]
    return pl.pallas_call(
        _mm_kernel,
        out_shape=jax.ShapeDtypeStruct((m, n), out_dtype),
        grid=(m // tm, n // tn),
        in_specs=[pl.BlockSpec((tm, k), lambda i, j: (i, 0)), pl.BlockSpec((k, tn), lambda i, j: (0, j))],
        out_specs=pl.BlockSpec((tm, tn), lambda i, j: (i, j)),
        compiler_params=_cparams("parallel", "parallel"),
        name="in_proj",
    )(a, w)


def _conv_kernel(ac_ref, ab_ref, ah_ref, ba_ref, bg_ref, ga_ref, gb_ref,
                 wa_ref, wb_ref, bb_ref, lng_ref, lnb_ref, wao_ref, wbo_ref,
                 out_ref, ua_ext, ub_ext, vb_scr):
    ts = out_ref.shape[1]
    width = ua_ext.shape[1]

    @pl.when(pl.program_id(1) == 0)
    def _():
        ua_ext[0:HALO, :] = jnp.zeros((HALO, width), F32)
        ub_ext[0:HALO, :] = jnp.zeros((HALO, width), F32)

    ua_ext[HALO:HALO + ts, :] = ac_ref[0].astype(F32) * ah_ref[0].astype(F32)
    ub_ext[HALO:HALO + ts, :] = ba_ref[0].astype(F32) * jax.nn.sigmoid(bg_ref[0].astype(F32))

    wa = wa_ref[...]
    va = wa[0:1] * ua_ext[HALO - 2:HALO - 2 + ts, :]
    va = va + wa[1:2] * ua_ext[HALO - 1:HALO - 1 + ts, :]
    va = va + wa[2:3] * ua_ext[HALO:HALO + ts, :]
    ya = jnp.dot((ab_ref[0].astype(F32) * va).astype(BF16), wao_ref[...], preferred_element_type=F32)

    wb = wb_ref[...]
    bias = jnp.broadcast_to(bb_ref[...], (CONV_ROWS, width))
    for c in range(ts // CONV_ROWS):
        acc = bias
        for k in range(CONV_B_K):
            r0 = HALO - (CONV_B_K - 1) + k + c * CONV_ROWS
            acc = acc + wb[k:k + 1] * ub_ext[r0:r0 + CONV_ROWS, :]
        vb_scr[c * CONV_ROWS:(c + 1) * CONV_ROWS, :] = acc
    u = vb_scr[...]
    mu = jnp.mean(u, axis=-1, keepdims=True)
    var = jnp.mean(jnp.square(u - mu), axis=-1, keepdims=True)
    y = (u - mu) * lax.rsqrt(var + EPS)
    y = y * lng_ref[...] + lnb_ref[...]
    y = y * jax.nn.sigmoid(y)
    yb = jnp.dot(y.astype(BF16), wbo_ref[...], preferred_element_type=F32)

    out_ref[0] = (jax.nn.sigmoid(ga_ref[0].astype(F32)) * ya
                  + jax.nn.sigmoid(gb_ref[0].astype(F32)) * yb)

    ua_ext[0:HALO, :] = ua_ext[ts:ts + HALO, :]
    ub_ext[0:HALO, :] = ub_ext[ts:ts + HALO, :]


def _conv_mixers(proj, wa, wb, bb, lng, lnb, wao, wbo, ts=512):
    b, s, _ = proj.shape
    w = CONV_A_WIDTH
    gate_blk = (GATE_COL + GATE_PAD) // D_MODEL

    def col(width, idx):
        return pl.BlockSpec((1, ts, width), lambda bi, si: (bi, si, idx))

    def full(arr):
        return pl.BlockSpec(arr.shape, lambda bi, si: (0,) * arr.ndim)

    return pl.pallas_call(
        _conv_kernel,
        out_shape=jax.ShapeDtypeStruct((b, s, D_MODEL), F32),
        grid=(b, s // ts),
        in_specs=[col(w, 0), col(w, 1), col(w, 2), col(w, 3), col(w, 4),
                  col(D_MODEL, gate_blk), col(D_MODEL, gate_blk + 1),
                  full(wa), full(wb), full(bb), full(lng), full(lnb), full(wao), full(wbo)],
        out_specs=pl.BlockSpec((1, ts, D_MODEL), lambda bi, si: (bi, si, 0)),
        scratch_shapes=[pltpu.VMEM((HALO + ts, w), F32), pltpu.VMEM((HALO + ts, w), F32),
                        pltpu.VMEM((ts, w), F32)],
        compiler_params=_cparams("parallel", "arbitrary"),
        name="conv_mixers",
    )(proj, proj, proj, proj, proj, proj, proj, wa, wb, bb, lng, lnb, wao, wbo)


def _prep_kernel(q_ref, kv_ref, gt_ref, cos_ref, sin_ref, qg_ref, kg_ref, mblk_ref,
                 qt_ref, kcmp_ref, vcmp_ref, ka_ref, vt_ref, go_ref):
    ts = q_ref.shape[1]
    ck = vt_ref.shape[5]
    cos = cos_ref[...]
    sin = sin_ref[...]
    mblk = mblk_ref[...]
    lane = lax.broadcasted_iota(jnp.int32, (ts, 128), 1)
    first_half = (lane & (HEAD_DIM - 1)) < HEAD_DIM // 2
    low = lane < HEAD_DIM
    tabs = pl.program_id(1) * ts + lax.broadcasted_iota(jnp.int32, (ts, 128), 0)
    blk_onehot = jnp.where(lane - HEAD_DIM == tabs // SLC_BLOCK, 1.0, 0.0)
    ones_rows = jnp.where(lax.broadcasted_iota(jnp.int32, (VT_ROWS - HEAD_DIM, ck), 0) == 0, 1.0, 0.0).astype(BF16)

    def per_group(x):
        return x, pltpu.roll(x, HEAD_DIM, 1)

    def norm_rope(x, g):
        xx = x * x
        hi = xx.astype(BF16)
        lo = (xx - hi.astype(F32)).astype(BF16)
        ms = (jnp.dot(hi, mblk, preferred_element_type=F32)
              + jnp.dot(lo, mblk, preferred_element_type=F32))
        y = x * lax.rsqrt(ms + EPS) * g
        partner = jnp.where(first_half, pltpu.roll(y, 128 - HEAD_DIM // 2, 1), pltpu.roll(y, HEAD_DIM // 2, 1))
        return y * cos + partner * sin

    qg = qg_ref[...]
    for j in range(ATTN_WIDTH // 128):
        o_t = (norm_rope(q_ref[0, :, 128 * j:128 * (j + 1)].astype(F32), qg) * QSCALE).T
        qt_ref[0, 2 * j] = o_t[:HEAD_DIM].astype(BF16)
        qt_ref[0, 2 * j + 1] = o_t[HEAD_DIM:].astype(BF16)
    for i in range(3):
        k = norm_rope(kv_ref[0, :, 256 * i:256 * i + 128].astype(F32), kg_ref[i:i + 1, :])
        v = kv_ref[0, :, 256 * i + 128:256 * i + 256].astype(F32)
        if i == 0:
            for g, (kg, vg) in enumerate(zip(per_group(k), per_group(v))):
                kcmp_ref[0, g] = kg[:, :HEAD_DIM].astype(BF16)
                vcmp_ref[0, g] = vg[:, :HEAD_DIM].astype(BF16)
        else:
            extra = blk_onehot if i == 1 else 0.0
            for g, kg in enumerate(per_group(k)):
                ka_ref[0, i - 1, g] = jnp.where(low, kg, extra).astype(BF16)
            v_t = v.T
            for g in range(N_KV):
                for c in range(ts // ck):
                    vt_ref[0, i - 1, g, c, :HEAD_DIM, :] = (
                        v_t[g * HEAD_DIM:(g + 1) * HEAD_DIM, c * ck:(c + 1) * ck].astype(BF16))
                    vt_ref[0, i - 1, g, c, HEAD_DIM:, :] = ones_rows
    sg = jax.nn.sigmoid(gt_ref[0].astype(F32))
    ngate = go_ref.shape[2]
    go_ref[0, 0] = sg.T[:ngate]
    go_ref[0, 1] = pltpu.roll(sg, 128 - GROUP * N_BRANCH, 1).T[:ngate]


def _attn_prep(proj, cos128, sin128, qg128, kg128, mblk, ts=512, ck=ATTN_CHUNK):
    b, s, _ = proj.shape

    def full(arr):
        return pl.BlockSpec(arr.shape, lambda bi, si: (0,) * arr.ndim)

    q_blk = (3 * CONV_A_WIDTH + 2 * CONV_B_WIDTH) // ATTN_WIDTH
    kv_blk = (q_blk + 1) * ATTN_WIDTH // (6 * KV_WIDTH)
    gt_blk = GATE_COL // 128
    return pl.pallas_call(
        _prep_kernel,
        out_shape=(jax.ShapeDtypeStruct((b, N_HEADS, HEAD_DIM, s), BF16),
                   jax.ShapeDtypeStruct((b, N_KV, s, HEAD_DIM), BF16),
                   jax.ShapeDtypeStruct((b, N_KV, s, HEAD_DIM), BF16),
                   jax.ShapeDtypeStruct((b, 2, N_KV, s, 128), BF16),
                   jax.ShapeDtypeStruct((b, 2, N_KV, s // ck, VT_ROWS, ck), BF16),
                   jax.ShapeDtypeStruct((b, N_KV, GATE_ROWS, s), F32)),
        grid=(b, s // ts),
        in_specs=[pl.BlockSpec((1, ts, ATTN_WIDTH), lambda bi, si: (bi, si, q_blk)),
                  pl.BlockSpec((1, ts, 6 * KV_WIDTH), lambda bi, si: (bi, si, kv_blk)),
                  pl.BlockSpec((1, ts, 128), lambda bi, si: (bi, si, gt_blk)),
                  pl.BlockSpec((ts, 128), lambda bi, si: (si, 0)),
                  pl.BlockSpec((ts, 128), lambda bi, si: (si, 0)),
                  full(qg128), full(kg128), full(mblk)],
        out_specs=(pl.BlockSpec((1, N_HEADS, HEAD_DIM, ts), lambda bi, si: (bi, 0, 0, si)),
                   pl.BlockSpec((1, N_KV, ts, HEAD_DIM), lambda bi, si: (bi, 0, si, 0)),
                   pl.BlockSpec((1, N_KV, ts, HEAD_DIM), lambda bi, si: (bi, 0, si, 0)),
                   pl.BlockSpec((1, 2, N_KV, ts, 128), lambda bi, si: (bi, 0, 0, si, 0)),
                   pl.BlockSpec((1, 2, N_KV, ts // ck, VT_ROWS, ck), lambda bi, si: (bi, 0, 0, si, 0, 0)),
                   pl.BlockSpec((1, N_KV, GATE_ROWS, ts), lambda bi, si: (bi, 0, 0, si))),
        compiler_params=_cparams("parallel", "parallel"),
        name="attn_prep",
    )(proj, proj, proj, cos128, sin128, qg128, kg128, mblk)


def _compress_kernel(k_ref, v_ref, pe_ref, w1_ref, b1_ref, w2_ref, kc_ref, vc_ref):
    nb = k_ref.shape[2]
    half = w1_ref.shape[1] // 2
    row = lax.broadcasted_iota(jnp.int32, (nb, w2_ref.shape[2]), 0)
    for i, (src, dst) in enumerate(((k_ref, kc_ref), (v_ref, vc_ref))):
        u = src[0, 0].astype(F32)
        top = (u + pe_ref[i, 0:1, :]).astype(BF16)
        bot = (u + pe_ref[i, 1:2, :]).astype(BF16)
        a = jnp.dot(top, w1_ref[i, :half, :], preferred_element_type=F32)
        bm = jnp.dot(bot, w1_ref[i, half:, :], preferred_element_type=F32)
        hid = a + pltpu.roll(bm, nb - 1, 0) + b1_ref[i]
        hid = jax.nn.gelu(hid)
        out = jnp.dot(hid.astype(BF16), w2_ref[i], preferred_element_type=F32)
        out = jnp.where(row < nb - 1, out, 0.0)
        dst[0, 0] = (out if i == 0 else out.T).astype(dst.dtype)


def _compress(k4, v4, pe, w1, b1, w2):
    b, _, nb, _ = k4.shape
    width = w2.shape[2]
    assert nb == width

    def full(arr):
        return pl.BlockSpec(arr.shape, lambda bi, gi: (0,) * arr.ndim)

    blk = pl.BlockSpec((1, 1, nb, CMP_STRIDE * HEAD_DIM), lambda bi, gi: (bi, gi, 0, 0))
    out_blk = pl.BlockSpec((1, 1, nb, width), lambda bi, gi: (bi, gi, 0, 0))
    return pl.pallas_call(
        _compress_kernel,
        out_shape=(jax.ShapeDtypeStruct((b, N_KV, nb, width), BF16),) * 2,
        grid=(b, N_KV),
        in_specs=[blk, blk, full(pe), full(w1), full(b1), full(w2)],
        out_specs=(out_blk, out_blk),
        compiler_params=_cparams("parallel", "parallel"),
        name="compress",
    )(k4, v4, pe, w1, b1, w2)


def _attn_kernel(q_ref, kc_ref, vct_ref, ks_ref, vs_ref, kw_ref, vw_ref, gt_ref, o_ref,
                 qa_scr, m_scr, acc_scr, oc_scr):
    tq = q_ref.shape[3]
    ck = vs_ref.shape[5]
    s_len = ks_ref.shape[3]
    nc = kc_ref.shape[2]
    n_sel = s_len // SLC_BLOCK
    cols = GROUP * tq
    qi = pl.program_id(2)
    t0 = qi * tq

    def head(r):
        return slice(r * tq, (r + 1) * tq)

    def keys(k_ref, chunk):
        return k_ref[0, 0, 0, pl.ds(pl.multiple_of(chunk * ck, ck), ck), :]

    def gate(branch):
        gt = gt_ref[0, 0]
        return jnp.concatenate([gt[N_BRANCH * r + branch:N_BRANCH * r + branch + 1] for r in range(GROUP)], axis=1)

    q0 = jnp.concatenate([jnp.concatenate([q_ref[0, r] for r in range(GROUP)], axis=1),
                          jnp.zeros((128 - HEAD_DIM, cols), BF16)], axis=0)
    qa_scr[...] = q0

    kj = lax.broadcasted_iota(jnp.int32, (ck, tq), 0)
    qc = lax.broadcasted_iota(jnp.int32, (ck, tq), 1)
    causal = jnp.where(kj <= qc, 0.0, NEG)
    tail = jnp.where(kj > qc, 0.0, NEG)

    n_win = WINDOW // ck
    win_chunks = [(qi, causal)]
    for d in range(1, n_win + 1):
        base = tail if d == n_win else jnp.zeros((ck, tq), F32)
        win_chunks.append((jnp.maximum(qi - d, 0), jnp.where(qi >= d, base, NEG)))
    sw = [jnp.dot(keys(kw_ref, c), q0, preferred_element_type=F32) + jnp.tile(msk, (1, GROUP))
          for c, msk in win_chunks]
    mw = functools.reduce(jnp.maximum, [jnp.max(s, axis=0, keepdims=True) for s in sw])
    accw = sum(jnp.dot(vw_ref[0, 0, 0, c], jnp.exp2(s - mw).astype(BF16), preferred_element_type=F32)
               for (c, _), s in zip(win_chunks, sw))
    o_win = accw[:HEAD_DIM] / accw[HEAD_DIM:HEAD_DIM + 1]

    tpos = t0 + (lax.broadcasted_iota(jnp.int32, (nc, cols), 1) & (tq - 1))
    cidx = lax.broadcasted_iota(jnp.int32, (nc, cols), 0)
    cmask = ((cidx * CMP_STRIDE + (CMP_BLOCK - 1)) <= tpos) & (cidx < nc - 1)
    sc = jnp.dot(kc_ref[0, 0], q0, preferred_element_type=F32)
    sc = jnp.where(cmask, sc, NEG)
    mc = jnp.max(sc, axis=0, keepdims=True)
    ec = jnp.where(cmask, jnp.exp2(sc - mc), 0.0)
    dc = jnp.sum(ec, axis=0, keepdims=True)
    pc = ec / jnp.where(dc > 0.0, dc, 1.0)
    o_cmp = jnp.dot(vct_ref[0, 0], pc.astype(BF16), preferred_element_type=F32)

    psum = pc[:, head(0)] + pc[:, head(1)] + pc[:, head(2)] + pc[:, head(3)]
    p_hi = psum.astype(BF16)
    p_lo = (psum - p_hi.astype(F32)).astype(BF16)
    jrow = lax.broadcasted_iota(jnp.int32, (n_sel, nc), 0)
    ccol = lax.broadcasted_iota(jnp.int32, (n_sel, nc), 1)
    ovl = ((ccol * CMP_STRIDE < jrow * SLC_BLOCK + SLC_BLOCK)
           & (ccol * CMP_STRIDE + CMP_BLOCK > jrow * SLC_BLOCK))
    ovl = jnp.where(ovl, 1.0, 0.0).astype(BF16)
    imp = (jnp.dot(ovl, p_hi, preferred_element_type=F32)
           + jnp.dot(ovl, p_lo, preferred_element_type=F32))

    jj = lax.broadcasted_iota(jnp.int32, (n_sel, tq), 0)
    cur = (t0 + lax.broadcasted_iota(jnp.int32, (n_sel, tq), 1)) // SLC_BLOCK
    valid = jj <= cur
    forced = (jj == 0) | (jj == cur) | (jj == cur - 1)
    score = jnp.where(forced, jnp.inf, jnp.where(valid, imp, -jnp.inf))
    rank = jnp.zeros((n_sel, tq), F32)
    for jp in range(n_sel):
        other = score[jp:jp + 1, :]
        ahead = (other > score) | ((other == score) & (jj > jp))
        rank = rank + jnp.where(ahead, 1.0, 0.0)
    sel = jnp.where(valid & (rank < float(min(SLC_TOPN, n_sel))), 1.0, 0.0)

    pen = jnp.where(sel > 0.5, 0.0, NEG).astype(BF16)
    for r in range(GROUP):
        qa_scr[HEAD_DIM:HEAD_DIM + n_sel, head(r)] = pen

    oc_scr[...] = gate(0) * o_cmp[:HEAD_DIM] + gate(2) * o_win

    def produce(chunk):
        return jnp.dot(keys(ks_ref, chunk), qa_scr[...], preferred_element_type=F32)

    def consume(s, chunk, mask):
        if mask is not None:
            s = s + jnp.tile(mask, (1, GROUP))
        m_prev = m_scr[...]
        m_new = jnp.maximum(m_prev, jnp.max(s, axis=0, keepdims=True))
        alpha = jnp.exp2(m_prev - m_new)
        p = jnp.exp2(s - m_new).astype(BF16)
        acc_scr[...] = alpha * acc_scr[...] + jnp.dot(vs_ref[0, 0, 0, chunk], p, preferred_element_type=F32)
        m_scr[...] = m_new

    m_scr[...] = jnp.full(m_scr.shape, NEG, F32)
    acc_scr[...] = jnp.zeros(acc_scr.shape, F32)

    def slc_body(c, s):
        s_next = produce(c + 1)
        consume(s, c, None)
        return s_next

    consume(lax.fori_loop(0, qi, slc_body, produce(0)), qi, causal)
    acc = acc_scr[...]
    o_slc = acc[:HEAD_DIM] / acc[HEAD_DIM:HEAD_DIM + 1]

    o_all = oc_scr[...] + gate(1) * o_slc
    o_ref[0] = jnp.concatenate([o_all[:, head(r)] for r in range(GROUP)], axis=0).T.astype(o_ref.dtype)


def _attention(qt, kc, vct, ka, vt, gates_t, tq=ATTN_CHUNK):
    b, _, _, s = qt.shape
    nc = kc.shape[2]
    ck = vt.shape[5]
    cols = GROUP * tq
    assert tq == ck and WINDOW % ck == 0 and tq & (tq - 1) == 0

    kseq = lambda branch: pl.BlockSpec((1, 1, 1, s, 128), lambda bi, gi, qi: (bi, branch, gi, 0, 0))
    vseq = lambda branch: pl.BlockSpec((1, 1, 1, s // ck, VT_ROWS, ck), lambda bi, gi, qi: (bi, branch, gi, 0, 0, 0))
    cblk = pl.BlockSpec((1, 1, nc, 128), lambda bi, gi, qi: (bi, gi, 0, 0))
    return pl.pallas_call(
        _attn_kernel,
        out_shape=jax.ShapeDtypeStruct((b, s, ATTN_WIDTH), BF16),
        grid=(b, N_KV, s // tq),
        in_specs=[pl.BlockSpec((1, GROUP, HEAD_DIM, tq), lambda bi, gi, qi: (bi, gi, 0, qi)),
                  cblk, cblk, kseq(0), vseq(0), kseq(1), vseq(1),
                  pl.BlockSpec((1, 1, GATE_ROWS, tq), lambda bi, gi, qi: (bi, gi, 0, qi))],
        out_specs=pl.BlockSpec((1, tq, GROUP * HEAD_DIM), lambda bi, gi, qi: (bi, qi, gi)),
        scratch_shapes=[pltpu.VMEM((128, cols), BF16),
                        pltpu.VMEM((1, cols), F32),
                        pltpu.VMEM((VT_ROWS, cols), F32),
                        pltpu.VMEM((HEAD_DIM, cols), F32)],
        compiler_params=_cparams("parallel", "parallel", "arbitrary"),
        name="nsa_attention",
    )(qt, kc, vct, ka, vt, ka, vt, gates_t)


def _merge_kernel(o_ref, mab_ref, gc_ref, x_ref, wn_ref, wo_ref, g2_ref, xo_ref, h_ref):
    yc = jnp.dot(o_ref[...], wn_ref[...], preferred_element_type=F32)
    mixed = mab_ref[...] + jax.nn.sigmoid(gc_ref[...].astype(F32)) * yc
    x = x_ref[...] + jnp.dot(mixed.astype(BF16), wo_ref[...], preferred_element_type=F32)
    xo_ref[...] = x
    ms = jnp.mean(x * x, axis=-1, keepdims=True)
    h_ref[...] = (x * lax.rsqrt(ms + EPS) * g2_ref[...]).astype(h_ref.dtype)


def _merge(o, mab, proj2d, x, wn, wo, g2, tm=512):
    t, d = x.shape
    gc_blk = (GATE_COL + GATE_PAD) // D_MODEL + 2

    def full(arr):
        return pl.BlockSpec(arr.shape, lambda i: (0,) * arr.ndim)

    return pl.pallas_call(
        _merge_kernel,
        out_shape=(jax.ShapeDtypeStruct((t, d), F32), jax.ShapeDtypeStruct((t, d), BF16)),
        grid=(t // tm,),
        in_specs=[pl.BlockSpec((tm, ATTN_WIDTH), lambda i: (i, 0)),
                  pl.BlockSpec((tm, d), lambda i: (i, 0)),
                  pl.BlockSpec((tm, d), lambda i: (i, gc_blk)),
                  pl.BlockSpec((tm, d), lambda i: (i, 0)),
                  full(wn), full(wo), full(g2)],
        out_specs=(pl.BlockSpec((tm, d), lambda i: (i, 0)), pl.BlockSpec((tm, d), lambda i: (i, 0))),
        compiler_params=_cparams("parallel"),
        name="merge_out_proj",
    )(o, mab, proj2d, x, wn, wo, g2)


def _ffn_kernel(h_ref, x_ref, w1_ref, w3_ref, w2_ref, gn_ref, xo_ref, hn_ref):
    c = pl.program_id(1)

    @pl.when(c == 0)
    def _():
        xo_ref[...] = x_ref[...]

    h = h_ref[...]
    u = jnp.dot(h, w1_ref[...], preferred_element_type=F32)
    v = jnp.dot(h, w3_ref[...], preferred_element_type=F32)
    act = (u * jax.nn.sigmoid(u) * v).astype(BF16)
    xo_ref[...] += jnp.dot(act, w2_ref[...], preferred_element_type=F32)

    @pl.when(c == pl.num_programs(1) - 1)
    def _():
        x = xo_ref[...]
        ms = jnp.mean(x * x, axis=-1, keepdims=True)
        hn_ref[...] = (x * lax.rsqrt(ms + EPS) * gn_ref[...]).astype(hn_ref.dtype)


def _ffn(h, x, w13, w2, g_next, tm=512, fc=1408):
    t, d = x.shape
    n_chunks = D_FF // fc
    return pl.pallas_call(
        _ffn_kernel,
        out_shape=(jax.ShapeDtypeStruct((t, d), F32), jax.ShapeDtypeStruct((t, d), BF16)),
        grid=(t // tm, n_chunks),
        in_specs=[pl.BlockSpec((tm, d), lambda i, c: (i, 0)),
                  pl.BlockSpec((tm, d), lambda i, c: (i, 0)),
                  pl.BlockSpec((d, fc), lambda i, c: (0, c)),
                  pl.BlockSpec((d, fc), lambda i, c: (0, c + n_chunks)),
                  pl.BlockSpec((fc, d), lambda i, c: (c, 0)),
                  pl.BlockSpec((1, d), lambda i, c: (0, 0))],
        out_specs=(pl.BlockSpec((tm, d), lambda i, c: (i, 0)), pl.BlockSpec((tm, d), lambda i, c: (i, 0))),
        compiler_params=_cparams("parallel", "arbitrary"),
        name="ffn",
    )(h, x, w13, w13, w2, g_next)


def _rope_tables(s):
    pos = jnp.arange(s, dtype=F32)
    inv = 1.0 / (ROPE_THETA ** (jnp.arange(0, HEAD_DIM, 2, dtype=F32) / HEAD_DIM))
    ang = pos[:, None] * inv[None, :]
    cos, sin = jnp.cos(ang), jnp.sin(ang)
    cos128 = jnp.concatenate([cos, cos, cos, cos], axis=-1)
    sin128 = jnp.concatenate([-sin, sin, -sin, sin], axis=-1)
    return cos128, sin128


def kernel(x, norm1_g, w_in, a_conv_w, a_w_out, b_conv_w, b_conv_b, b_ln_g, b_ln_b, b_w_out,
           q_norm_g, k_norm_g, cmp_pe, cmp_w1, cmp_b1, cmp_w2, nsa_w_out, w_o, norm2_g,
           ffn_w13, ffn_w2):
    b, s, d = x.shape
    depth = w_in.shape[0]
    t = b * s
    cos128, sin128 = _rope_tables(s)
    hid = jnp.arange(128) // HEAD_DIM
    mblk = jnp.where(hid[:, None] == hid[None, :], 1.0 / HEAD_DIM, 0.0).astype(BF16)

    w_in_p = jnp.concatenate(
        [w_in[:, :, :GATE_COL + N_GATE],
         jnp.zeros((depth, d, GATE_PAD - N_GATE), w_in.dtype),
         w_in[:, :, GATE_COL + N_GATE:]], axis=-1).astype(BF16)
    a_w_out_b = a_w_out.astype(BF16)
    b_w_out_b = b_w_out.astype(BF16)
    cmp_w1_b = cmp_w1.astype(BF16)
    cmp_w2_b = jnp.pad(cmp_w2, ((0, 0), (0, 0), (0, 0), (0, 128 - HEAD_DIM))).astype(BF16)
    nsa_w_out_b = nsa_w_out.astype(BF16)
    w_o_b = w_o.astype(BF16)
    ffn_w13_b = ffn_w13.astype(BF16)
    ffn_w2_b = ffn_w2.astype(BF16)
    qg128 = jnp.concatenate([q_norm_g, q_norm_g], axis=-1)[:, None, :]
    kg128 = jnp.concatenate([k_norm_g, k_norm_g], axis=-1)
    pe2 = cmp_pe.reshape(depth, 2, 2, (CMP_BLOCK // 2) * HEAD_DIM)
    b1 = cmp_b1[:, :, None, :]

    xf = x.reshape(t, d)
    h = _rmsnorm(xf, norm1_g[0:1])
    for l in range(depth):
        proj = _matmul(h, w_in_p[l], BF16)
        proj3 = proj.reshape(b, s, PROJ_WIDTH)
        mab = _conv_mixers(proj3, a_conv_w[l], b_conv_w[l], b_conv_b[l:l + 1], b_ln_g[l:l + 1],
                           b_ln_b[l:l + 1], a_w_out_b[l], b_w_out_b[l])
        qt, kcmp, vcmp, ka, vt, gates_t = _attn_prep(proj3, cos128, sin128, qg128[l], kg128[l], mblk)
        nb = s // CMP_STRIDE
        kc, vct = _compress(kcmp.reshape(b, N_KV, nb, CMP_STRIDE * HEAD_DIM),
                            vcmp.reshape(b, N_KV, nb, CMP_STRIDE * HEAD_DIM),
                            pe2[l], cmp_w1_b[l], b1[l], cmp_w2_b[l])
        o = _attention(qt, kc, vct, ka, vt, gates_t)
        xf, h2 = _merge(o.reshape(t, ATTN_WIDTH), mab.reshape(t, d), proj, xf,
                        nsa_w_out_b[l], w_o_b[l], norm2_g[l:l + 1])
        g_next = norm1_g[l + 1:l + 2] if l + 1 < depth else norm1_g[l:l + 1]
        xf, h = _ffn(h2, xf, ffn_w13_b[l], ffn_w2_b[l], g_next)
    return xf.reshape(b, s, d)
```

```python
import functools

import jax
import jax.numpy as jnp
from jax import lax
from jax.experimental import pallas as pl
from jax.experimental.pallas import tpu as pltpu

F32 = jnp.float32
BF16 = jnp.bfloat16

D_MODEL = 1024
DEPTH = 4
CONV_A_WIDTH = 512
CONV_A_K = 3
CONV_B_WIDTH = 512
CONV_B_K = 31
N_HEADS = 8
N_KV = 2
HEAD_DIM = 64
GROUP = N_HEADS // N_KV
ATTN_WIDTH = N_HEADS * HEAD_DIM
KV_WIDTH = N_KV * HEAD_DIM
N_BRANCH = 3
CMP_BLOCK = 32
CMP_STRIDE = 16
CMP_HIDDEN = 256
SLC_BLOCK = 64
SLC_TOPN = 16
WINDOW = 512
ROPE_THETA = 10000.0
D_FF = -(-8 * D_MODEL // (3 * 256)) * 256
EPS = 1e-6
NEG = -1e30
SCALE = HEAD_DIM ** -0.5
QSCALE = SCALE * 1.4426950408889634

N_GATE = N_HEADS * N_BRANCH
GATE_COL = 3 * CONV_A_WIDTH + 2 * CONV_B_WIDTH + ATTN_WIDTH + 6 * KV_WIDTH
GATE_PAD = 256
PROJ_WIDTH = GATE_COL + GATE_PAD + 3 * D_MODEL

MXU_COLS = 256
VMEM_LIMIT = 56 * 1024 * 1024
HALO = 32
CONV_ROWS = 32
ATTN_CHUNK = 256
GATE_ROWS = 16
VT_ROWS = HEAD_DIM + 16


def _cparams(*sem):
    return pltpu.CompilerParams(dimension_semantics=sem, vmem_limit_bytes=VMEM_LIMIT)


def _rmsnorm_kernel(x_ref, g_ref, h_ref):
    x = x_ref[...]
    ms = jnp.mean(x * x, axis=-1, keepdims=True)
    h_ref[...] = (x * lax.rsqrt(ms + EPS) * g_ref[...]).astype(h_ref.dtype)


def _rmsnorm(x, g, tm=1024):
    t, d = x.shape
    return pl.pallas_call(
        _rmsnorm_kernel,
        out_shape=jax.ShapeDtypeStruct((t, d), BF16),
        grid=(t // tm,),
        in_specs=[pl.BlockSpec((tm, d), lambda i: (i, 0)), pl.BlockSpec((1, d), lambda i: (0, 0))],
        out_specs=pl.BlockSpec((tm, d), lambda i: (i, 0)),
        compiler_params=_cparams("parallel"),
        name="rmsnorm",
    )(x, g)


def _layer_spec(arr, l, resident=False):
    zeros = (0,) * (arr.ndim - 1)
    return pl.BlockSpec((1,) + arr.shape[1:], lambda *_: (l,) + zeros,
                        pipeline_mode=pl.Buffered(1) if resident else None)


def _in_proj_kernel(h_ref, *refs, chunk):
    w_refs, o_ref = refs[:-1], refs[-1]
    h = h_ref[...]
    col = 0
    for w_ref in w_refs:
        n = w_ref.shape[2]
        for lo in range(0, n, chunk):
            hi = min(lo + chunk, n)
            o_ref[:, col + lo:col + hi] = jnp.dot(
                h, w_ref[0, :, lo:hi], preferred_element_type=F32).astype(o_ref.dtype)
        col += n


def _in_proj(h, w_segments, l, tm=512, chunk=1280):
    m, k = h.shape
    n = sum(w.shape[2] for w in w_segments)
    return pl.pallas_call(
        functools.partial(_in_proj_kernel, chunk=chunk),
        out_shape=jax.ShapeDtypeStruct((m, n), BF16),
        grid=(m // tm,),
        in_specs=[pl.BlockSpec((tm, k), lambda i: (i, 0))] + [_layer_spec(w, l, resident=True) for w in w_segments],
        out_specs=pl.BlockSpec((tm, n), lambda i: (i, 0)),
        compiler_params=_cparams("parallel"),
        name="in_proj",
    )(h, *w_segments)


def _conv_kernel(ac_ref, ab_ref, ah_ref, ba_ref, bg_ref, ga_ref, gb_ref,
                 wa_ref, wb_ref, bb_ref, lng_ref, lnb_ref, wao_ref, wbo_ref,
                 out_ref, ua_ext, ub_ext, vb_scr):
    ts = out_ref.shape[1]
    width = ua_ext.shape[1]

    @pl.when(pl.program_id(1) == 0)
    def _():
        ua_ext[0:HALO, :] = jnp.zeros((HALO, width), F32)
        ub_ext[0:HALO, :] = jnp.zeros((HALO, width), F32)

    ua_ext[HALO:HALO + ts, :] = ac_ref[0].astype(F32) * ah_ref[0].astype(F32)
    ub_ext[HALO:HALO + ts, :] = ba_ref[0].astype(F32) * jax.nn.sigmoid(bg_ref[0].astype(F32))

    wa = wa_ref[...]
    va = wa[0:1] * ua_ext[HALO - 2:HALO - 2 + ts, :]
    va = va + wa[1:2] * ua_ext[HALO - 1:HALO - 1 + ts, :]
    va = va + wa[2:3] * ua_ext[HALO:HALO + ts, :]
    ya = jnp.dot((ab_ref[0].astype(F32) * va).astype(BF16), wao_ref[0], preferred_element_type=F32)

    wb = wb_ref[...]
    bias = jnp.broadcast_to(bb_ref[...], (CONV_ROWS, width))
    for c in range(ts // CONV_ROWS):
        acc = bias
        for k in range(CONV_B_K):
            r0 = HALO - (CONV_B_K - 1) + k + c * CONV_ROWS
            acc = acc + wb[k:k + 1] * ub_ext[r0:r0 + CONV_ROWS, :]
        vb_scr[c * CONV_ROWS:(c + 1) * CONV_ROWS, :] = acc
    u = vb_scr[...]
    mu = jnp.mean(u, axis=-1, keepdims=True)
    var = jnp.mean(jnp.square(u - mu), axis=-1, keepdims=True)
    y = (u - mu) * lax.rsqrt(var + EPS)
    y = y * lng_ref[...] + lnb_ref[...]
    y = y * jax.nn.sigmoid(y)
    yb = jnp.dot(y.astype(BF16), wbo_ref[0], preferred_element_type=F32)

    out_ref[0] = (jax.nn.sigmoid(ga_ref[0].astype(F32)) * ya
                  + jax.nn.sigmoid(gb_ref[0].astype(F32)) * yb)

    ua_ext[0:HALO, :] = ua_ext[ts:ts + HALO, :]
    ub_ext[0:HALO, :] = ub_ext[ts:ts + HALO, :]


def _conv_mixers(proj, wa, wb, bb, lng, lnb, wao, wbo, l, ts=512):
    b, s, _ = proj.shape
    w = CONV_A_WIDTH
    gate_blk = (GATE_COL + GATE_PAD) // D_MODEL

    def col(width, idx):
        return pl.BlockSpec((1, ts, width), lambda bi, si: (bi, si, idx))

    def full(arr):
        return pl.BlockSpec(arr.shape, lambda bi, si: (0,) * arr.ndim)

    return pl.pallas_call(
        _conv_kernel,
        out_shape=jax.ShapeDtypeStruct((b, s, D_MODEL), F32),
        grid=(b, s // ts),
        in_specs=[col(w, 0), col(w, 1), col(w, 2), col(w, 3), col(w, 4),
                  col(D_MODEL, gate_blk), col(D_MODEL, gate_blk + 1),
                  full(wa), full(wb), full(bb), full(lng), full(lnb), _layer_spec(wao, l), _layer_spec(wbo, l)],
        out_specs=pl.BlockSpec((1, ts, D_MODEL), lambda bi, si: (bi, si, 0)),
        scratch_shapes=[pltpu.VMEM((HALO + ts, w), F32), pltpu.VMEM((HALO + ts, w), F32),
                        pltpu.VMEM((ts, w), F32)],
        compiler_params=_cparams("parallel", "arbitrary"),
        name="conv_mixers",
    )(proj, proj, proj, proj, proj, proj, proj, wa, wb, bb, lng, lnb, wao, wbo)


def _prep_kernel(q_ref, kv_ref, gt_ref, cos_ref, sin_ref, qg_ref, kg_ref, mblk_ref,
                 qt_ref, kcmp_ref, vcmp_ref, ka_ref, vt_ref, go_ref):
    ts = q_ref.shape[1]
    ck = vt_ref.shape[5]
    cos = cos_ref[...]
    sin = sin_ref[...]
    mblk = mblk_ref[...]
    lane = lax.broadcasted_iota(jnp.int32, (ts, 128), 1)
    first_half = (lane & (HEAD_DIM - 1)) < HEAD_DIM // 2
    low = lane < HEAD_DIM
    tabs = pl.program_id(1) * ts + lax.broadcasted_iota(jnp.int32, (ts, 128), 0)
    blk_onehot = jnp.where(lane - HEAD_DIM == tabs // SLC_BLOCK, 1.0, 0.0)
    ones_rows = jnp.where(lax.broadcasted_iota(jnp.int32, (VT_ROWS - HEAD_DIM, ck), 0) == 0, 1.0, 0.0).astype(BF16)

    def per_group(x):
        return x, pltpu.roll(x, HEAD_DIM, 1)

    def norm_rope(x, g):
        xx = x * x
        hi = xx.astype(BF16)
        lo = (xx - hi.astype(F32)).astype(BF16)
        ms = (jnp.dot(hi, mblk, preferred_element_type=F32)
              + jnp.dot(lo, mblk, preferred_element_type=F32))
        y = x * lax.rsqrt(ms + EPS) * g
        partner = jnp.where(first_half, pltpu.roll(y, 128 - HEAD_DIM // 2, 1), pltpu.roll(y, HEAD_DIM // 2, 1))
        return y * cos + partner * sin

    qg = qg_ref[...]
    for j in range(ATTN_WIDTH // 128):
        o_t = (norm_rope(q_ref[0, :, 128 * j:128 * (j + 1)].astype(F32), qg) * QSCALE).T
        qt_ref[0, 2 * j] = o_t[:HEAD_DIM].astype(BF16)
        qt_ref[0, 2 * j + 1] = o_t[HEAD_DIM:].astype(BF16)
    for i in range(3):
        k = norm_rope(kv_ref[0, :, 256 * i:256 * i + 128].astype(F32), kg_ref[i:i + 1, :])
        v = kv_ref[0, :, 256 * i + 128:256 * i + 256].astype(F32)
        if i == 0:
            for g, (kg, vg) in enumerate(zip(per_group(k), per_group(v))):
                kcmp_ref[0, g] = kg[:, :HEAD_DIM].astype(BF16)
                vcmp_ref[0, g] = vg[:, :HEAD_DIM].astype(BF16)
        else:
            extra = blk_onehot if i == 1 else 0.0
            for g, kg in enumerate(per_group(k)):
                ka_ref[0, i - 1, g] = jnp.where(low, kg, extra).astype(BF16)
            v_t = v.T
            for g in range(N_KV):
                for c in range(ts // ck):
                    vt_ref[0, i - 1, g, c, :HEAD_DIM, :] = (
                        v_t[g * HEAD_DIM:(g + 1) * HEAD_DIM, c * ck:(c + 1) * ck].astype(BF16))
                    vt_ref[0, i - 1, g, c, HEAD_DIM:, :] = ones_rows
    sg = jax.nn.sigmoid(gt_ref[0].astype(F32))
    ngate = go_ref.shape[2]
    go_ref[0, 0] = sg.T[:ngate]
    go_ref[0, 1] = pltpu.roll(sg, 128 - GROUP * N_BRANCH, 1).T[:ngate]


def _attn_prep(proj, cos128, sin128, qg128, kg128, mblk, ts=512, ck=ATTN_CHUNK):
    b, s, _ = proj.shape

    def full(arr):
        return pl.BlockSpec(arr.shape, lambda bi, si: (0,) * arr.ndim)

    q_blk = (3 * CONV_A_WIDTH + 2 * CONV_B_WIDTH) // ATTN_WIDTH
    kv_blk = (q_blk + 1) * ATTN_WIDTH // (6 * KV_WIDTH)
    gt_blk = GATE_COL // 128
    return pl.pallas_call(
        _prep_kernel,
        out_shape=(jax.ShapeDtypeStruct((b, N_HEADS, HEAD_DIM, s), BF16),
                   jax.ShapeDtypeStruct((b, N_KV, s, HEAD_DIM), BF16),
                   jax.ShapeDtypeStruct((b, N_KV, s, HEAD_DIM), BF16),
                   jax.ShapeDtypeStruct((b, 2, N_KV, s, 128), BF16),
                   jax.ShapeDtypeStruct((b, 2, N_KV, s // ck, VT_ROWS, ck), BF16),
                   jax.ShapeDtypeStruct((b, N_KV, GATE_ROWS, s), F32)),
        grid=(b, s // ts),
        in_specs=[pl.BlockSpec((1, ts, ATTN_WIDTH), lambda bi, si: (bi, si, q_blk)),
                  pl.BlockSpec((1, ts, 6 * KV_WIDTH), lambda bi, si: (bi, si, kv_blk)),
                  pl.BlockSpec((1, ts, 128), lambda bi, si: (bi, si, gt_blk)),
                  pl.BlockSpec((ts, 128), lambda bi, si: (si, 0)),
                  pl.BlockSpec((ts, 128), lambda bi, si: (si, 0)),
                  full(qg128), full(kg128), full(mblk)],
        out_specs=(pl.BlockSpec((1, N_HEADS, HEAD_DIM, ts), lambda bi, si: (bi, 0, 0, si)),
                   pl.BlockSpec((1, N_KV, ts, HEAD_DIM), lambda bi, si: (bi, 0, si, 0)),
                   pl.BlockSpec((1, N_KV, ts, HEAD_DIM), lambda bi, si: (bi, 0, si, 0)),
                   pl.BlockSpec((1, 2, N_KV, ts, 128), lambda bi, si: (bi, 0, 0, si, 0)),
                   pl.BlockSpec((1, 2, N_KV, ts // ck, VT_ROWS, ck), lambda bi, si: (bi, 0, 0, si, 0, 0)),
                   pl.BlockSpec((1, N_KV, GATE_ROWS, ts), lambda bi, si: (bi, 0, 0, si))),
        compiler_params=_cparams("parallel", "parallel"),
        name="attn_prep",
    )(proj, proj, proj, cos128, sin128, qg128, kg128, mblk)


def _compress_kernel(k_ref, v_ref, pe_ref, w1_ref, b1_ref, w2_ref, kc_ref, vc_ref):
    nb = k_ref.shape[2]
    half = w1_ref.shape[2] // 2
    row = lax.broadcasted_iota(jnp.int32, (nb, w2_ref.shape[2]), 0)
    for i, (src, dst) in enumerate(((k_ref, kc_ref), (v_ref, vc_ref))):
        u = src[0, 0].astype(F32)
        top = (u + pe_ref[i, 0:1, :]).astype(BF16)
        bot = (u + pe_ref[i, 1:2, :]).astype(BF16)
        a = jnp.dot(top, w1_ref[0, i, :half, :], preferred_element_type=F32)
        bm = jnp.dot(bot, w1_ref[0, i, half:, :], preferred_element_type=F32)
        hid = a + pltpu.roll(bm, nb - 1, 0) + b1_ref[i]
        hid = jax.nn.gelu(hid)
        out = jnp.dot(hid.astype(BF16), w2_ref[i], preferred_element_type=F32)
        out = jnp.where(row < nb - 1, out, 0.0)
        dst[0, 0] = (out if i == 0 else out.T).astype(dst.dtype)


def _compress(k4, v4, pe, w1, l, b1, w2):
    b, _, nb, _ = k4.shape
    width = w2.shape[2]
    assert nb == width

    def full(arr):
        return pl.BlockSpec(arr.shape, lambda bi, gi: (0,) * arr.ndim)

    blk = pl.BlockSpec((1, 1, nb, CMP_STRIDE * HEAD_DIM), lambda bi, gi: (bi, gi, 0, 0))
    out_blk = pl.BlockSpec((1, 1, nb, width), lambda bi, gi: (bi, gi, 0, 0))
    return pl.pallas_call(
        _compress_kernel,
        out_shape=(jax.ShapeDtypeStruct((b, N_KV, nb, width), BF16),) * 2,
        grid=(b, N_KV),
        in_specs=[blk, blk, full(pe), _layer_spec(w1, l), full(b1), full(w2)],
        out_specs=(out_blk, out_blk),
        compiler_params=_cparams("parallel", "parallel"),
        name="compress",
    )(k4, v4, pe, w1, b1, w2)


def _attn_kernel(q_ref, kc_ref, vct_ref, ks_ref, vs_ref, kw_ref, vw_ref, gt_ref, o_ref,
                 qa_scr, m_scr, acc_scr, oc_scr):
    tq = q_ref.shape[3]
    ck = vs_ref.shape[5]
    s_len = ks_ref.shape[3]
    nc = kc_ref.shape[2]
    n_sel = s_len // SLC_BLOCK
    cols = GROUP * tq
    qi = pl.program_id(2)
    t0 = qi * tq

    def head(r):
        return slice(r * tq, (r + 1) * tq)

    def keys(k_ref, chunk):
        return k_ref[0, 0, 0, pl.ds(pl.multiple_of(chunk * ck, ck), ck), :]

    def gate(branch):
        gt = gt_ref[0, 0]
        return jnp.concatenate([gt[N_BRANCH * r + branch:N_BRANCH * r + branch + 1] for r in range(GROUP)], axis=1)

    q0 = jnp.concatenate([jnp.concatenate([q_ref[0, r] for r in range(GROUP)], axis=1),
                          jnp.zeros((128 - HEAD_DIM, cols), BF16)], axis=0)
    qa_scr[...] = q0

    kj = lax.broadcasted_iota(jnp.int32, (ck, tq), 0)
    qc = lax.broadcasted_iota(jnp.int32, (ck, tq), 1)
    causal = jnp.where(kj <= qc, 0.0, NEG)
    tail = jnp.where(kj > qc, 0.0, NEG)

    n_win = WINDOW // ck
    win_chunks = [(qi, causal)]
    for d in range(1, n_win + 1):
        base = tail if d == n_win else jnp.zeros((ck, tq), F32)
        win_chunks.append((jnp.maximum(qi - d, 0), jnp.where(qi >= d, base, NEG)))
    sw = [jnp.dot(keys(kw_ref, c), q0, preferred_element_type=F32) + jnp.tile(msk, (1, GROUP))
          for c, msk in win_chunks]
    mw = functools.reduce(jnp.maximum, [jnp.max(s, axis=0, keepdims=True) for s in sw])
    accw = sum(jnp.dot(vw_ref[0, 0, 0, c], jnp.exp2(s - mw).astype(BF16), preferred_element_type=F32)
               for (c, _), s in zip(win_chunks, sw))
    o_win = accw[:HEAD_DIM] / accw[HEAD_DIM:HEAD_DIM + 1]

    tpos = t0 + (lax.broadcasted_iota(jnp.int32, (nc, cols), 1) & (tq - 1))
    cidx = lax.broadcasted_iota(jnp.int32, (nc, cols), 0)
    cmask = ((cidx * CMP_STRIDE + (CMP_BLOCK - 1)) <= tpos) & (cidx < nc - 1)
    sc = jnp.dot(kc_ref[0, 0], q0, preferred_element_type=F32)
    sc = jnp.where(cmask, sc, NEG)
    mc = jnp.max(sc, axis=0, keepdims=True)
    ec = jnp.where(cmask, jnp.exp2(sc - mc), 0.0)
    dc = jnp.sum(ec, axis=0, keepdims=True)
    pc = ec / jnp.where(dc > 0.0, dc, 1.0)
    o_cmp = jnp.dot(vct_ref[0, 0], pc.astype(BF16), preferred_element_type=F32)

    psum = pc[:, head(0)] + pc[:, head(1)] + pc[:, head(2)] + pc[:, head(3)]
    p_hi = psum.astype(BF16)
    p_lo = (psum - p_hi.astype(F32)).astype(BF16)
    jrow = lax.broadcasted_iota(jnp.int32, (n_sel, nc), 0)
    ccol = lax.broadcasted_iota(jnp.int32, (n_sel, nc), 1)
    ovl = ((ccol * CMP_STRIDE < jrow * SLC_BLOCK + SLC_BLOCK)
           & (ccol * CMP_STRIDE + CMP_BLOCK > jrow * SLC_BLOCK))
    ovl = jnp.where(ovl, 1.0, 0.0).astype(BF16)
    imp = (jnp.dot(ovl, p_hi, preferred_element_type=F32)
           + jnp.dot(ovl, p_lo, preferred_element_type=F32))

    jj = lax.broadcasted_iota(jnp.int32, (n_sel, tq), 0)
    cur = (t0 + lax.broadcasted_iota(jnp.int32, (n_sel, tq), 1)) // SLC_BLOCK
    valid = jj <= cur
    forced = (jj == 0) | (jj == cur) | (jj == cur - 1)
    score = jnp.where(forced, jnp.inf, jnp.where(valid, imp, -jnp.inf))
    rank = jnp.zeros((n_sel, tq), F32)
    for jp in range(n_sel):
        other = score[jp:jp + 1, :]
        ahead = (other > score) | ((other == score) & (jj > jp))
        rank = rank + jnp.where(ahead, 1.0, 0.0)
    sel = jnp.where(valid & (rank < float(min(SLC_TOPN, n_sel))), 1.0, 0.0)

    pen = jnp.where(sel > 0.5, 0.0, NEG).astype(BF16)
    for r in range(GROUP):
        qa_scr[HEAD_DIM:HEAD_DIM + n_sel, head(r)] = pen

    oc_scr[...] = gate(0) * o_cmp[:HEAD_DIM] + gate(2) * o_win

    def produce(chunk):
        return jnp.dot(keys(ks_ref, chunk), qa_scr[...], preferred_element_type=F32)

    def consume(s, chunk, mask):
        if mask is not None:
            s = s + jnp.tile(mask, (1, GROUP))
        m_prev = m_scr[...]
        m_new = jnp.maximum(m_prev, jnp.max(s, axis=0, keepdims=True))
        alpha = jnp.exp2(m_prev - m_new)
        p = jnp.exp2(s - m_new).astype(BF16)
        acc_scr[...] = alpha * acc_scr[...] + jnp.dot(vs_ref[0, 0, 0, chunk], p, preferred_element_type=F32)
        m_scr[...] = m_new

    m_scr[...] = jnp.full(m_scr.shape, NEG, F32)
    acc_scr[...] = jnp.zeros(acc_scr.shape, F32)

    def slc_body(c, s):
        s_next = produce(c + 1)
        consume(s, c, None)
        return s_next

    consume(lax.fori_loop(0, qi, slc_body, produce(0)), qi, causal)
    acc = acc_scr[...]
    o_slc = acc[:HEAD_DIM] / acc[HEAD_DIM:HEAD_DIM + 1]

    o_all = oc_scr[...] + gate(1) * o_slc
    o_ref[0] = jnp.concatenate([o_all[:, head(r)] for r in range(GROUP)], axis=0).T.astype(o_ref.dtype)


def _attention(qt, kc, vct, ka, vt, gates_t, tq=ATTN_CHUNK):
    b, _, _, s = qt.shape
    nc = kc.shape[2]
    ck = vt.shape[5]
    cols = GROUP * tq
    assert tq == ck and WINDOW % ck == 0 and tq & (tq - 1) == 0

    kseq = lambda branch: pl.BlockSpec((1, 1, 1, s, 128), lambda bi, gi, qi: (bi, branch, gi, 0, 0))
    vseq = lambda branch: pl.BlockSpec((1, 1, 1, s // ck, VT_ROWS, ck), lambda bi, gi, qi: (bi, branch, gi, 0, 0, 0))
    cblk = pl.BlockSpec((1, 1, nc, 128), lambda bi, gi, qi: (bi, gi, 0, 0))
    return pl.pallas_call(
        _attn_kernel,
        out_shape=jax.ShapeDtypeStruct((b, s, ATTN_WIDTH), BF16),
        grid=(b, N_KV, s // tq),
        in_specs=[pl.BlockSpec((1, GROUP, HEAD_DIM, tq), lambda bi, gi, qi: (bi, gi, 0, qi)),
                  cblk, cblk, kseq(0), vseq(0), kseq(1), vseq(1),
                  pl.BlockSpec((1, 1, GATE_ROWS, tq), lambda bi, gi, qi: (bi, gi, 0, qi))],
        out_specs=pl.BlockSpec((1, tq, GROUP * HEAD_DIM), lambda bi, gi, qi: (bi, qi, gi)),
        scratch_shapes=[pltpu.VMEM((128, cols), BF16),
                        pltpu.VMEM((1, cols), F32),
                        pltpu.VMEM((VT_ROWS, cols), F32),
                        pltpu.VMEM((HEAD_DIM, cols), F32)],
        compiler_params=_cparams("parallel", "parallel", "arbitrary"),
        name="nsa_attention",
    )(qt, kc, vct, ka, vt, ka, vt, gates_t)


def _merge_kernel(o_ref, mab_ref, gc_ref, x_ref, wn_ref, wo_ref, g2_ref, xo_ref, h_ref):
    yc = jnp.dot(o_ref[...], wn_ref[0], preferred_element_type=F32)
    mixed = mab_ref[...] + jax.nn.sigmoid(gc_ref[...].astype(F32)) * yc
    x = x_ref[...] + jnp.dot(mixed.astype(BF16), wo_ref[0], preferred_element_type=F32)
    xo_ref[...] = x
    ms = jnp.mean(x * x, axis=-1, keepdims=True)
    h_ref[...] = (x * lax.rsqrt(ms + EPS) * g2_ref[...]).astype(h_ref.dtype)


def _merge(o, mab, proj2d, x, wn, wo, l, g2, tm=512):
    t, d = x.shape
    gc_blk = (GATE_COL + GATE_PAD) // D_MODEL + 2

    def full(arr):
        return pl.BlockSpec(arr.shape, lambda i: (0,) * arr.ndim)

    return pl.pallas_call(
        _merge_kernel,
        out_shape=(jax.ShapeDtypeStruct((t, d), F32), jax.ShapeDtypeStruct((t, d), BF16)),
        grid=(t // tm,),
        in_specs=[pl.BlockSpec((tm, ATTN_WIDTH), lambda i: (i, 0)),
                  pl.BlockSpec((tm, d), lambda i: (i, 0)),
                  pl.BlockSpec((tm, d), lambda i: (i, gc_blk)),
                  pl.BlockSpec((tm, d), lambda i: (i, 0)),
                  _layer_spec(wn, l), _layer_spec(wo, l), full(g2)],
        out_specs=(pl.BlockSpec((tm, d), lambda i: (i, 0)), pl.BlockSpec((tm, d), lambda i: (i, 0))),
        compiler_params=_cparams("parallel"),
        name="merge_out_proj",
    )(o, mab, proj2d, x, wn, wo, g2)


def _ffn_kernel(h_ref, x_ref, w13_ref, w2_ref, gn_ref, xo_ref, hn_ref, *, bounds):
    h = h_ref[...]
    x = x_ref[...]
    for lo, hi in bounds:
        u = jnp.dot(h, w13_ref[0, :, lo:hi], preferred_element_type=F32)
        v = jnp.dot(h, w13_ref[0, :, D_FF + lo:D_FF + hi], preferred_element_type=F32)
        act = (u * jax.nn.sigmoid(u) * v).astype(BF16)
        x = x + jnp.dot(act, w2_ref[0, lo:hi, :], preferred_element_type=F32)
    xo_ref[...] = x
    ms = jnp.mean(x * x, axis=-1, keepdims=True)
    hn_ref[...] = (x * lax.rsqrt(ms + EPS) * gn_ref[...]).astype(hn_ref.dtype)


def _ffn(h, x, w13, w2, l, g_next, tm=512, n_chunks=2):
    t, d = x.shape
    tiles = D_FF // MXU_COLS
    edges = [MXU_COLS * (tiles * i // n_chunks) for i in range(n_chunks + 1)]
    return pl.pallas_call(
        functools.partial(_ffn_kernel, bounds=tuple(zip(edges[:-1], edges[1:]))),
        out_shape=(jax.ShapeDtypeStruct((t, d), F32), jax.ShapeDtypeStruct((t, d), BF16)),
        grid=(t // tm,),
        in_specs=[pl.BlockSpec((tm, d), lambda i: (i, 0)),
                  pl.BlockSpec((tm, d), lambda i: (i, 0)),
                  _layer_spec(w13, l, resident=True), _layer_spec(w2, l, resident=True),
                  pl.BlockSpec((1, d), lambda i: (0, 0))],
        out_specs=(pl.BlockSpec((tm, d), lambda i: (i, 0)), pl.BlockSpec((tm, d), lambda i: (i, 0))),
        compiler_params=_cparams("parallel"),
        name="ffn",
    )(h, x, w13, w2, g_next)


def _rope_tables(s):
    pos = jnp.arange(s, dtype=F32)
    inv = 1.0 / (ROPE_THETA ** (jnp.arange(0, HEAD_DIM, 2, dtype=F32) / HEAD_DIM))
    ang = pos[:, None] * inv[None, :]
    cos, sin = jnp.cos(ang), jnp.sin(ang)
    cos128 = jnp.concatenate([cos, cos, cos, cos], axis=-1)
    sin128 = jnp.concatenate([-sin, sin, -sin, sin], axis=-1)
    return cos128, sin128


def kernel(x, norm1_g, w_in, a_conv_w, a_w_out, b_conv_w, b_conv_b, b_ln_g, b_ln_b, b_w_out,
           q_norm_g, k_norm_g, cmp_pe, cmp_w1, cmp_b1, cmp_w2, nsa_w_out, w_o, norm2_g,
           ffn_w13, ffn_w2):
    b, s, d = x.shape
    depth = w_in.shape[0]
    t = b * s
    cos128, sin128 = _rope_tables(s)
    hid = jnp.arange(128) // HEAD_DIM
    mblk = jnp.where(hid[:, None] == hid[None, :], 1.0 / HEAD_DIM, 0.0).astype(BF16)

    w_in_segs = (w_in[:, :, :GATE_COL].astype(BF16),
                 jnp.pad(w_in[:, :, GATE_COL:GATE_COL + N_GATE],
                         ((0, 0), (0, 0), (0, GATE_PAD - N_GATE))).astype(BF16),
                 w_in[:, :, GATE_COL + N_GATE:].astype(BF16))
    a_w_out_b = a_w_out.astype(BF16)
    b_w_out_b = b_w_out.astype(BF16)
    cmp_w1_b = cmp_w1.astype(BF16)
    cmp_w2_b = jnp.pad(cmp_w2, ((0, 0), (0, 0), (0, 0), (0, 128 - HEAD_DIM))).astype(BF16)
    nsa_w_out_b = nsa_w_out.astype(BF16)
    w_o_b = w_o.astype(BF16)
    ffn_w13_b = ffn_w13.astype(BF16)
    ffn_w2_b = ffn_w2.astype(BF16)
    qg128 = jnp.concatenate([q_norm_g, q_norm_g], axis=-1)[:, None, :]
    kg128 = jnp.concatenate([k_norm_g, k_norm_g], axis=-1)
    pe2 = cmp_pe.reshape(depth, 2, 2, (CMP_BLOCK // 2) * HEAD_DIM)
    b1 = cmp_b1[:, :, None, :]

    xf = x.reshape(t, d)
    h = _rmsnorm(xf, norm1_g[0:1])
    for l in range(depth):
        proj = _in_proj(h, w_in_segs, l)
        proj3 = proj.reshape(b, s, PROJ_WIDTH)
        mab = _conv_mixers(proj3, a_conv_w[l], b_conv_w[l], b_conv_b[l:l + 1], b_ln_g[l:l + 1],
                           b_ln_b[l:l + 1], a_w_out_b, b_w_out_b, l)
        qt, kcmp, vcmp, ka, vt, gates_t = _attn_prep(proj3, cos128, sin128, qg128[l], kg128[l], mblk)
        nb = s // CMP_STRIDE
        kc, vct = _compress(kcmp.reshape(b, N_KV, nb, CMP_STRIDE * HEAD_DIM),
                            vcmp.reshape(b, N_KV, nb, CMP_STRIDE * HEAD_DIM),
                            pe2[l], cmp_w1_b, l, b1[l], cmp_w2_b[l])
        o = _attention(qt, kc, vct, ka, vt, gates_t)
        xf, h2 = _merge(o.reshape(t, ATTN_WIDTH), mab.reshape(t, d), proj, xf,
                        nsa_w_out_b, w_o_b, l, norm2_g[l:l + 1])
        g_next = norm1_g[l + 1:l + 2] if l + 1 < depth else norm1_g[l:l + 1]
        xf, h = _ffn(h2, xf, ffn_w13_b, ffn_w2_b, l, g_next)
    return xf.reshape(b, s, d)
```

```python
import functools

import jax
import jax.numpy as jnp
from jax import lax
from jax.experimental import pallas as pl
from jax.experimental.pallas import tpu as pltpu

F32 = jnp.float32
BF16 = jnp.bfloat16

D_MODEL = 1024
DEPTH = 4
CONV_A_WIDTH = 512
CONV_A_K = 3
CONV_B_WIDTH = 512
CONV_B_K = 31
N_HEADS = 8
N_KV = 2
HEAD_DIM = 64
GROUP = N_HEADS // N_KV
ATTN_WIDTH = N_HEADS * HEAD_DIM
KV_WIDTH = N_KV * HEAD_DIM
N_BRANCH = 3
CMP_BLOCK = 32
CMP_STRIDE = 16
CMP_HIDDEN = 256
SLC_BLOCK = 64
SLC_TOPN = 16
WINDOW = 512
ROPE_THETA = 10000.0
D_FF = -(-8 * D_MODEL // (3 * 256)) * 256
EPS = 1e-6
NEG = -1e30
SCALE = HEAD_DIM ** -0.5
QSCALE = SCALE * 1.4426950408889634

N_GATE = N_HEADS * N_BRANCH
GATE_COL = 3 * CONV_A_WIDTH + 2 * CONV_B_WIDTH + ATTN_WIDTH + 6 * KV_WIDTH
GATE_PAD = 256
PROJ_WIDTH = GATE_COL + GATE_PAD + 3 * D_MODEL

MXU_COLS = 256
VMEM_LIMIT = 56 * 1024 * 1024
HALO = 32
CONV_ROWS = 32
ATTN_CHUNK = 256
GATE_ROWS = 16
VT_ROWS = HEAD_DIM + 16


def _cparams(*sem):
    return pltpu.CompilerParams(dimension_semantics=sem, vmem_limit_bytes=VMEM_LIMIT)


def _rmsnorm_kernel(x_ref, g_ref, h_ref):
    x = x_ref[...]
    ms = jnp.mean(x * x, axis=-1, keepdims=True)
    h_ref[...] = (x * lax.rsqrt(ms + EPS) * g_ref[...]).astype(h_ref.dtype)


def _rmsnorm(x, g, tm=1024):
    t, d = x.shape
    return pl.pallas_call(
        _rmsnorm_kernel,
        out_shape=jax.ShapeDtypeStruct((t, d), BF16),
        grid=(t // tm,),
        in_specs=[pl.BlockSpec((tm, d), lambda i: (i, 0)), pl.BlockSpec((1, d), lambda i: (0, 0))],
        out_specs=pl.BlockSpec((tm, d), lambda i: (i, 0)),
        compiler_params=_cparams("parallel"),
        name="rmsnorm",
    )(x, g)


def _layer_spec(arr, l, resident=False):
    zeros = (0,) * (arr.ndim - 1)
    return pl.BlockSpec((1,) + arr.shape[1:], lambda *_: (l,) + zeros,
                        pipeline_mode=pl.Buffered(1) if resident else None)


def _in_proj_kernel(h_ref, *refs, chunk):
    w_refs, o_ref = refs[:-1], refs[-1]
    h = h_ref[...]
    col = 0
    for w_ref in w_refs:
        n = w_ref.shape[2]
        for lo in range(0, n, chunk):
            hi = min(lo + chunk, n)
            o_ref[:, col + lo:col + hi] = jnp.dot(
                h, w_ref[0, :, lo:hi], preferred_element_type=F32).astype(o_ref.dtype)
        col += n


def _in_proj(h, w_segments, l, tm=512, chunk=1280):
    m, k = h.shape
    n = sum(w.shape[2] for w in w_segments)
    return pl.pallas_call(
        functools.partial(_in_proj_kernel, chunk=chunk),
        out_shape=jax.ShapeDtypeStruct((m, n), BF16),
        grid=(m // tm,),
        in_specs=[pl.BlockSpec((tm, k), lambda i: (i, 0))] + [_layer_spec(w, l, resident=True) for w in w_segments],
        out_specs=pl.BlockSpec((tm, n), lambda i: (i, 0)),
        compiler_params=_cparams("parallel"),
        name="in_proj",
    )(h, *w_segments)


def _conv_kernel(ac_ref, ab_ref, ah_ref, ba_ref, bg_ref, ga_ref, gb_ref,
                 wa_ref, wb_ref, bb_ref, lng_ref, lnb_ref, wao_ref, wbo_ref,
                 out_ref, ua_ext, ub_ext, vb_scr):
    ts = out_ref.shape[1]
    width = ua_ext.shape[1]

    @pl.when(pl.program_id(1) == 0)
    def _():
        ua_ext[0:HALO, :] = jnp.zeros((HALO, width), F32)
        ub_ext[0:HALO, :] = jnp.zeros((HALO, width), F32)

    ua_ext[HALO:HALO + ts, :] = ac_ref[0].astype(F32) * ah_ref[0].astype(F32)
    ub_ext[HALO:HALO + ts, :] = ba_ref[0].astype(F32) * jax.nn.sigmoid(bg_ref[0].astype(F32))

    wa = wa_ref[...]
    va = wa[0:1] * ua_ext[HALO - 2:HALO - 2 + ts, :]
    va = va + wa[1:2] * ua_ext[HALO - 1:HALO - 1 + ts, :]
    va = va + wa[2:3] * ua_ext[HALO:HALO + ts, :]
    ya = jnp.dot((ab_ref[0].astype(F32) * va).astype(BF16), wao_ref[0], preferred_element_type=F32)

    wb = wb_ref[...]
    bias = jnp.broadcast_to(bb_ref[...], (CONV_ROWS, width))
    for c in range(ts // CONV_ROWS):
        acc = bias
        for k in range(CONV_B_K):
            r0 = HALO - (CONV_B_K - 1) + k + c * CONV_ROWS
            acc = acc + wb[k:k + 1] * ub_ext[r0:r0 + CONV_ROWS, :]
        vb_scr[c * CONV_ROWS:(c + 1) * CONV_ROWS, :] = acc
    u = vb_scr[...]
    mu = jnp.mean(u, axis=-1, keepdims=True)
    var = jnp.mean(jnp.square(u - mu), axis=-1, keepdims=True)
    y = (u - mu) * lax.rsqrt(var + EPS)
    y = y * lng_ref[...] + lnb_ref[...]
    y = y * jax.nn.sigmoid(y)
    yb = jnp.dot(y.astype(BF16), wbo_ref[0], preferred_element_type=F32)

    out_ref[0] = (jax.nn.sigmoid(ga_ref[0].astype(F32)) * ya
                  + jax.nn.sigmoid(gb_ref[0].astype(F32)) * yb)

    ua_ext[0:HALO, :] = ua_ext[ts:ts + HALO, :]
    ub_ext[0:HALO, :] = ub_ext[ts:ts + HALO, :]


def _conv_mixers(proj, wa, wb, bb, lng, lnb, wao, wbo, l, ts=512):
    b, s, _ = proj.shape
    w = CONV_A_WIDTH
    gate_blk = (GATE_COL + GATE_PAD) // D_MODEL

    def col(width, idx):
        return pl.BlockSpec((1, ts, width), lambda bi, si: (bi, si, idx))

    def full(arr):
        return pl.BlockSpec(arr.shape, lambda bi, si: (0,) * arr.ndim)

    return pl.pallas_call(
        _conv_kernel,
        out_shape=jax.ShapeDtypeStruct((b, s, D_MODEL), F32),
        grid=(b, s // ts),
        in_specs=[col(w, 0), col(w, 1), col(w, 2), col(w, 3), col(w, 4),
                  col(D_MODEL, gate_blk), col(D_MODEL, gate_blk + 1),
                  full(wa), full(wb), full(bb), full(lng), full(lnb), _layer_spec(wao, l), _layer_spec(wbo, l)],
        out_specs=pl.BlockSpec((1, ts, D_MODEL), lambda bi, si: (bi, si, 0)),
        scratch_shapes=[pltpu.VMEM((HALO + ts, w), F32), pltpu.VMEM((HALO + ts, w), F32),
                        pltpu.VMEM((ts, w), F32)],
        compiler_params=_cparams("parallel", "arbitrary"),
        name="conv_mixers",
    )(proj, proj, proj, proj, proj, proj, proj, wa, wb, bb, lng, lnb, wao, wbo)


def _prep_kernel(q_ref, kv_ref, gt_ref, cos_ref, sin_ref, qg_ref, kg_ref, mblk_ref,
                 qt_ref, kcmp_ref, vcmp_ref, ka_ref, vt_ref, go_ref):
    ts = q_ref.shape[1]
    ck = vt_ref.shape[5]
    cos = cos_ref[...]
    sin = sin_ref[...]
    mblk = mblk_ref[...]
    lane = lax.broadcasted_iota(jnp.int32, (ts, 128), 1)
    first_half = (lane & (HEAD_DIM - 1)) < HEAD_DIM // 2
    low = lane < HEAD_DIM
    tabs = pl.program_id(1) * ts + lax.broadcasted_iota(jnp.int32, (ts, 128), 0)
    blk_onehot = jnp.where(lane - HEAD_DIM == tabs // SLC_BLOCK, 1.0, 0.0)
    ones_rows = jnp.where(lax.broadcasted_iota(jnp.int32, (VT_ROWS - HEAD_DIM, ck), 0) == 0, 1.0, 0.0).astype(BF16)

    def per_group(x):
        return x, pltpu.roll(x, HEAD_DIM, 1)

    def norm_rope(x, g):
        xx = x * x
        hi = xx.astype(BF16)
        lo = (xx - hi.astype(F32)).astype(BF16)
        ms = (jnp.dot(hi, mblk, preferred_element_type=F32)
              + jnp.dot(lo, mblk, preferred_element_type=F32))
        y = x * lax.rsqrt(ms + EPS) * g
        partner = jnp.where(first_half, pltpu.roll(y, 128 - HEAD_DIM // 2, 1), pltpu.roll(y, HEAD_DIM // 2, 1))
        return y * cos + partner * sin

    qg = qg_ref[...]
    for j in range(ATTN_WIDTH // 128):
        o_t = (norm_rope(q_ref[0, :, 128 * j:128 * (j + 1)].astype(F32), qg) * QSCALE).T
        qt_ref[0, 2 * j] = o_t[:HEAD_DIM].astype(BF16)
        qt_ref[0, 2 * j + 1] = o_t[HEAD_DIM:].astype(BF16)
    for i in range(3):
        k = norm_rope(kv_ref[0, :, 256 * i:256 * i + 128].astype(F32), kg_ref[i:i + 1, :])
        v = kv_ref[0, :, 256 * i + 128:256 * i + 256].astype(F32)
        if i == 0:
            for g, (kg, vg) in enumerate(zip(per_group(k), per_group(v))):
                kcmp_ref[0, g] = kg[:, :HEAD_DIM].astype(BF16)
                vcmp_ref[0, g] = vg[:, :HEAD_DIM].astype(BF16)
        else:
            extra = blk_onehot if i == 1 else 0.0
            for g, kg in enumerate(per_group(k)):
                ka_ref[0, i - 1, g] = jnp.where(low, kg, extra).astype(BF16)
            v_t = v.T
            for g in range(N_KV):
                for c in range(ts // ck):
                    vt_ref[0, i - 1, g, c, :HEAD_DIM, :] = (
                        v_t[g * HEAD_DIM:(g + 1) * HEAD_DIM, c * ck:(c + 1) * ck].astype(BF16))
                    vt_ref[0, i - 1, g, c, HEAD_DIM:, :] = ones_rows
    sg = jax.nn.sigmoid(gt_ref[0].astype(F32))
    ngate = go_ref.shape[2]
    go_ref[0, 0] = sg.T[:ngate]
    go_ref[0, 1] = pltpu.roll(sg, 128 - GROUP * N_BRANCH, 1).T[:ngate]


def _attn_prep(proj, cos128, sin128, qg128, kg128, mblk, ts=512, ck=ATTN_CHUNK):
    b, s, _ = proj.shape

    def full(arr):
        return pl.BlockSpec(arr.shape, lambda bi, si: (0,) * arr.ndim)

    q_blk = (3 * CONV_A_WIDTH + 2 * CONV_B_WIDTH) // ATTN_WIDTH
    kv_blk = (q_blk + 1) * ATTN_WIDTH // (6 * KV_WIDTH)
    gt_blk = GATE_COL // 128
    return pl.pallas_call(
        _prep_kernel,
        out_shape=(jax.ShapeDtypeStruct((b, N_HEADS, HEAD_DIM, s), BF16),
                   jax.ShapeDtypeStruct((b, N_KV, s, HEAD_DIM), BF16),
                   jax.ShapeDtypeStruct((b, N_KV, s, HEAD_DIM), BF16),
                   jax.ShapeDtypeStruct((b, 2, N_KV, s, 128), BF16),
                   jax.ShapeDtypeStruct((b, 2, N_KV, s // ck, VT_ROWS, ck), BF16),
                   jax.ShapeDtypeStruct((b, N_KV, GATE_ROWS, s), F32)),
        grid=(b, s // ts),
        in_specs=[pl.BlockSpec((1, ts, ATTN_WIDTH), lambda bi, si: (bi, si, q_blk)),
                  pl.BlockSpec((1, ts, 6 * KV_WIDTH), lambda bi, si: (bi, si, kv_blk)),
                  pl.BlockSpec((1, ts, 128), lambda bi, si: (bi, si, gt_blk)),
                  pl.BlockSpec((ts, 128), lambda bi, si: (si, 0)),
                  pl.BlockSpec((ts, 128), lambda bi, si: (si, 0)),
                  full(qg128), full(kg128), full(mblk)],
        out_specs=(pl.BlockSpec((1, N_HEADS, HEAD_DIM, ts), lambda bi, si: (bi, 0, 0, si)),
                   pl.BlockSpec((1, N_KV, ts, HEAD_DIM), lambda bi, si: (bi, 0, si, 0)),
                   pl.BlockSpec((1, N_KV, ts, HEAD_DIM), lambda bi, si: (bi, 0, si, 0)),
                   pl.BlockSpec((1, 2, N_KV, ts, 128), lambda bi, si: (bi, 0, 0, si, 0)),
                   pl.BlockSpec((1, 2, N_KV, ts // ck, VT_ROWS, ck), lambda bi, si: (bi, 0, 0, si, 0, 0)),
                   pl.BlockSpec((1, N_KV, GATE_ROWS, ts), lambda bi, si: (bi, 0, 0, si))),
        compiler_params=_cparams("parallel", "parallel"),
        name="attn_prep",
    )(proj, proj, proj, cos128, sin128, qg128, kg128, mblk)


def _compress_kernel(k_ref, v_ref, pe_ref, w1_ref, b1_ref, w2_ref, kc_ref, vc_ref):
    nb = k_ref.shape[2]
    half = w1_ref.shape[2] // 2
    row = lax.broadcasted_iota(jnp.int32, (nb, w2_ref.shape[2]), 0)
    for i, (src, dst) in enumerate(((k_ref, kc_ref), (v_ref, vc_ref))):
        u = src[0, 0].astype(F32)
        top = (u + pe_ref[i, 0:1, :]).astype(BF16)
        bot = (u + pe_ref[i, 1:2, :]).astype(BF16)
        a = jnp.dot(top, w1_ref[0, i, :half, :], preferred_element_type=F32)
        bm = jnp.dot(bot, w1_ref[0, i, half:, :], preferred_element_type=F32)
        hid = a + pltpu.roll(bm, nb - 1, 0) + b1_ref[i]
        hid = jax.nn.gelu(hid)
        out = jnp.dot(hid.astype(BF16), w2_ref[i], preferred_element_type=F32)
        out = jnp.where(row < nb - 1, out, 0.0)
        dst[0, 0] = (out if i == 0 else out.T).astype(dst.dtype)


def _compress(k4, v4, pe, w1, l, b1, w2):
    b, _, nb, _ = k4.shape
    width = w2.shape[2]
    assert nb == width

    def full(arr):
        return pl.BlockSpec(arr.shape, lambda bi, gi: (0,) * arr.ndim)

    blk = pl.BlockSpec((1, 1, nb, CMP_STRIDE * HEAD_DIM), lambda bi, gi: (bi, gi, 0, 0))
    out_blk = pl.BlockSpec((1, 1, nb, width), lambda bi, gi: (bi, gi, 0, 0))
    return pl.pallas_call(
        _compress_kernel,
        out_shape=(jax.ShapeDtypeStruct((b, N_KV, nb, width), BF16),) * 2,
        grid=(b, N_KV),
        in_specs=[blk, blk, full(pe), _layer_spec(w1, l), full(b1), full(w2)],
        out_specs=(out_blk, out_blk),
        compiler_params=_cparams("parallel", "parallel"),
        name="compress",
    )(k4, v4, pe, w1, b1, w2)


def _attn_kernel(q_ref, kc_ref, vct_ref, ks_ref, vs_ref, kw_ref, vw_ref, gt_ref, o_ref,
                 qa_scr, s_scr, m_scr, acc_scr, oc_scr):
    tq = q_ref.shape[3]
    ck = vs_ref.shape[5]
    s_len = ks_ref.shape[3]
    nc = kc_ref.shape[2]
    n_sel = s_len // SLC_BLOCK
    cols = GROUP * tq
    qi = pl.program_id(2)
    t0 = qi * tq

    def head(r):
        return slice(r * tq, (r + 1) * tq)

    def keys(k_ref, chunk):
        return k_ref[0, 0, 0, pl.ds(pl.multiple_of(chunk * ck, ck), ck), :]

    def gate(branch):
        gt = gt_ref[0, 0]
        return jnp.concatenate([gt[N_BRANCH * r + branch:N_BRANCH * r + branch + 1] for r in range(GROUP)], axis=1)

    q0 = jnp.concatenate([jnp.concatenate([q_ref[0, r] for r in range(GROUP)], axis=1),
                          jnp.zeros((128 - HEAD_DIM, cols), BF16)], axis=0)
    qa_scr[...] = q0

    kj = lax.broadcasted_iota(jnp.int32, (ck, tq), 0)
    qc = lax.broadcasted_iota(jnp.int32, (ck, tq), 1)
    causal = jnp.where(kj <= qc, 0.0, NEG)
    tail = jnp.where(kj > qc, 0.0, NEG)

    def produce(k_ref, chunk, qmat, slot):
        s_scr[slot] = jnp.dot(keys(k_ref, chunk), qmat, preferred_element_type=F32)

    def consume(slot, v_ref, chunk, mask):
        s = s_scr[slot]
        if mask is not None:
            s = s + jnp.tile(mask, (1, GROUP))
        m_prev = m_scr[...]
        m_new = jnp.maximum(m_prev, jnp.max(s, axis=0, keepdims=True))
        alpha = jnp.exp2(m_prev - m_new)
        p = jnp.exp2(s - m_new).astype(BF16)
        acc_scr[...] = alpha * acc_scr[...] + jnp.dot(v_ref[0, 0, 0, chunk], p, preferred_element_type=F32)
        m_scr[...] = m_new

    def reset():
        m_scr[...] = jnp.full(m_scr.shape, NEG, F32)
        acc_scr[...] = jnp.zeros(acc_scr.shape, F32)

    def finish():
        acc = acc_scr[...]
        return acc[:HEAD_DIM] / acc[HEAD_DIM:HEAD_DIM + 1]

    n_win = WINDOW // ck
    win_chunks = [(qi, causal)]
    for d in range(1, n_win + 1):
        base = tail if d == n_win else jnp.zeros((ck, tq), F32)
        win_chunks.append((jnp.maximum(qi - d, 0), jnp.where(qi >= d, base, NEG)))
    win_slot0 = 2
    for i, (c, _) in enumerate(win_chunks):
        produce(kw_ref, c, q0, win_slot0 + i)

    tpos = t0 + (lax.broadcasted_iota(jnp.int32, (nc, cols), 1) & (tq - 1))
    cidx = lax.broadcasted_iota(jnp.int32, (nc, cols), 0)
    cmask = ((cidx * CMP_STRIDE + (CMP_BLOCK - 1)) <= tpos) & (cidx < nc - 1)
    sc = jnp.dot(kc_ref[0, 0], q0, preferred_element_type=F32)
    sc = jnp.where(cmask, sc, NEG)
    mc = jnp.max(sc, axis=0, keepdims=True)
    ec = jnp.where(cmask, jnp.exp2(sc - mc), 0.0)
    dc = jnp.sum(ec, axis=0, keepdims=True)
    pc = ec / jnp.where(dc > 0.0, dc, 1.0)
    o_cmp = jnp.dot(vct_ref[0, 0], pc.astype(BF16), preferred_element_type=F32)

    psum = pc[:, head(0)] + pc[:, head(1)] + pc[:, head(2)] + pc[:, head(3)]
    p_hi = psum.astype(BF16)
    p_lo = (psum - p_hi.astype(F32)).astype(BF16)
    jrow = lax.broadcasted_iota(jnp.int32, (n_sel, nc), 0)
    ccol = lax.broadcasted_iota(jnp.int32, (n_sel, nc), 1)
    ovl = ((ccol * CMP_STRIDE < jrow * SLC_BLOCK + SLC_BLOCK)
           & (ccol * CMP_STRIDE + CMP_BLOCK > jrow * SLC_BLOCK))
    ovl = jnp.where(ovl, 1.0, 0.0).astype(BF16)
    imp = (jnp.dot(ovl, p_hi, preferred_element_type=F32)
           + jnp.dot(ovl, p_lo, preferred_element_type=F32))

    jj = lax.broadcasted_iota(jnp.int32, (n_sel, tq), 0)
    cur = (t0 + lax.broadcasted_iota(jnp.int32, (n_sel, tq), 1)) // SLC_BLOCK
    valid = jj <= cur
    forced = (jj == 0) | (jj == cur) | (jj == cur - 1)
    score = jnp.where(forced, jnp.inf, jnp.where(valid, imp, -jnp.inf))
    rank = jnp.zeros((n_sel, tq), F32)
    for jp in range(n_sel):
        other = score[jp:jp + 1, :]
        ahead = (other > score) | ((other == score) & (jj > jp))
        rank = rank + jnp.where(ahead, 1.0, 0.0)
    sel = jnp.where(valid & (rank < float(min(SLC_TOPN, n_sel))), 1.0, 0.0)

    pen = jnp.where(sel > 0.5, 0.0, NEG).astype(BF16)
    for r in range(GROUP):
        qa_scr[HEAD_DIM:HEAD_DIM + n_sel, head(r)] = pen

    produce(ks_ref, 0, qa_scr[...], 0)

    reset()
    for i, (c, msk) in enumerate(win_chunks):
        consume(win_slot0 + i, vw_ref, c, msk)
    oc_scr[...] = gate(0) * o_cmp[:HEAD_DIM] + gate(2) * finish()

    reset()

    def pair_body(j, carry):
        produce(ks_ref, 2 * j + 1, qa_scr[...], 1)
        consume(0, vs_ref, 2 * j, None)
        produce(ks_ref, 2 * j + 2, qa_scr[...], 0)
        consume(1, vs_ref, 2 * j + 1, None)
        return carry

    lax.fori_loop(0, qi // 2, pair_body, 0)

    @pl.when(qi % 2 == 1)
    def _():
        produce(ks_ref, qi, qa_scr[...], 1)
        consume(0, vs_ref, qi - 1, None)
        consume(1, vs_ref, qi, causal)

    @pl.when(qi % 2 == 0)
    def _():
        consume(0, vs_ref, qi, causal)

    o_slc = finish()

    o_all = oc_scr[...] + gate(1) * o_slc
    o_ref[0] = jnp.concatenate([o_all[:, head(r)] for r in range(GROUP)], axis=0).T.astype(o_ref.dtype)


def _attention(qt, kc, vct, ka, vt, gates_t, tq=ATTN_CHUNK):
    b, _, _, s = qt.shape
    nc = kc.shape[2]
    ck = vt.shape[5]
    cols = GROUP * tq
    assert tq == ck and WINDOW % ck == 0 and tq & (tq - 1) == 0

    kseq = lambda branch: pl.BlockSpec((1, 1, 1, s, 128), lambda bi, gi, qi: (bi, branch, gi, 0, 0))
    vseq = lambda branch: pl.BlockSpec((1, 1, 1, s // ck, VT_ROWS, ck), lambda bi, gi, qi: (bi, branch, gi, 0, 0, 0))
    cblk = pl.BlockSpec((1, 1, nc, 128), lambda bi, gi, qi: (bi, gi, 0, 0))
    return pl.pallas_call(
        _attn_kernel,
        out_shape=jax.ShapeDtypeStruct((b, s, ATTN_WIDTH), BF16),
        grid=(b, N_KV, s // tq),
        in_specs=[pl.BlockSpec((1, GROUP, HEAD_DIM, tq), lambda bi, gi, qi: (bi, gi, 0, qi)),
                  cblk, cblk, kseq(0), vseq(0), kseq(1), vseq(1),
                  pl.BlockSpec((1, 1, GATE_ROWS, tq), lambda bi, gi, qi: (bi, gi, 0, qi))],
        out_specs=pl.BlockSpec((1, tq, GROUP * HEAD_DIM), lambda bi, gi, qi: (bi, qi, gi)),
        scratch_shapes=[pltpu.VMEM((128, cols), BF16),
                        pltpu.VMEM((WINDOW // ck + 3, ck, cols), F32),
                        pltpu.VMEM((1, cols), F32),
                        pltpu.VMEM((VT_ROWS, cols), F32),
                        pltpu.VMEM((HEAD_DIM, cols), F32)],
        compiler_params=_cparams("parallel", "parallel", "arbitrary"),
        name="nsa_attention",
    )(qt, kc, vct, ka, vt, ka, vt, gates_t)


def _merge_kernel(o_ref, mab_ref, gc_ref, x_ref, wn_ref, wo_ref, g2_ref, xo_ref, h_ref):
    yc = jnp.dot(o_ref[...], wn_ref[0], preferred_element_type=F32)
    mixed = mab_ref[...] + jax.nn.sigmoid(gc_ref[...].astype(F32)) * yc
    x = x_ref[...] + jnp.dot(mixed.astype(BF16), wo_ref[0], preferred_element_type=F32)
    xo_ref[...] = x
    ms = jnp.mean(x * x, axis=-1, keepdims=True)
    h_ref[...] = (x * lax.rsqrt(ms + EPS) * g2_ref[...]).astype(h_ref.dtype)


def _merge(o, mab, proj2d, x, wn, wo, l, g2, tm=512):
    t, d = x.shape
    gc_blk = (GATE_COL + GATE_PAD) // D_MODEL + 2

    def full(arr):
        return pl.BlockSpec(arr.shape, lambda i: (0,) * arr.ndim)

    return pl.pallas_call(
        _merge_kernel,
        out_shape=(jax.ShapeDtypeStruct((t, d), F32), jax.ShapeDtypeStruct((t, d), BF16)),
        grid=(t // tm,),
        in_specs=[pl.BlockSpec((tm, ATTN_WIDTH), lambda i: (i, 0)),
                  pl.BlockSpec((tm, d), lambda i: (i, 0)),
                  pl.BlockSpec((tm, d), lambda i: (i, gc_blk)),
                  pl.BlockSpec((tm, d), lambda i: (i, 0)),
                  _layer_spec(wn, l), _layer_spec(wo, l), full(g2)],
        out_specs=(pl.BlockSpec((tm, d), lambda i: (i, 0)), pl.BlockSpec((tm, d), lambda i: (i, 0))),
        compiler_params=_cparams("parallel"),
        name="merge_out_proj",
    )(o, mab, proj2d, x, wn, wo, g2)


def _ffn_kernel(h_ref, x_ref, w13_ref, w2_ref, gn_ref, xo_ref, hn_ref, *, bounds):
    h = h_ref[...]
    x = x_ref[...]
    for lo, hi in bounds:
        u = jnp.dot(h, w13_ref[0, :, lo:hi], preferred_element_type=F32)
        v = jnp.dot(h, w13_ref[0, :, D_FF + lo:D_FF + hi], preferred_element_type=F32)
        act = (u * jax.nn.sigmoid(u) * v).astype(BF16)
        x = x + jnp.dot(act, w2_ref[0, lo:hi, :], preferred_element_type=F32)
    xo_ref[...] = x
    ms = jnp.mean(x * x, axis=-1, keepdims=True)
    hn_ref[...] = (x * lax.rsqrt(ms + EPS) * gn_ref[...]).astype(hn_ref.dtype)


def _ffn(h, x, w13, w2, l, g_next, tm=512, n_chunks=2):
    t, d = x.shape
    tiles = D_FF // MXU_COLS
    edges = [MXU_COLS * (tiles * i // n_chunks) for i in range(n_chunks + 1)]
    return pl.pallas_call(
        functools.partial(_ffn_kernel, bounds=tuple(zip(edges[:-1], edges[1:]))),
        out_shape=(jax.ShapeDtypeStruct((t, d), F32), jax.ShapeDtypeStruct((t, d), BF16)),
        grid=(t // tm,),
        in_specs=[pl.BlockSpec((tm, d), lambda i: (i, 0)),
                  pl.BlockSpec((tm, d), lambda i: (i, 0)),
                  _layer_spec(w13, l, resident=True), _layer_spec(w2, l, resident=True),
                  pl.BlockSpec((1, d), lambda i: (0, 0))],
        out_specs=(pl.BlockSpec((tm, d), lambda i: (i, 0)), pl.BlockSpec((tm, d), lambda i: (i, 0))),
        compiler_params=_cparams("parallel"),
        name="ffn",
    )(h, x, w13, w2, g_next)


def _rope_tables(s):
    pos = jnp.arange(s, dtype=F32)
    inv = 1.0 / (ROPE_THETA ** (jnp.arange(0, HEAD_DIM, 2, dtype=F32) / HEAD_DIM))
    ang = pos[:, None] * inv[None, :]
    cos, sin = jnp.cos(ang), jnp.sin(ang)
    cos128 = jnp.concatenate([cos, cos, cos, cos], axis=-1)
    sin128 = jnp.concatenate([-sin, sin, -sin, sin], axis=-1)
    return cos128, sin128


def kernel(x, norm1_g, w_in, a_conv_w, a_w_out, b_conv_w, b_conv_b, b_ln_g, b_ln_b, b_w_out,
           q_norm_g, k_norm_g, cmp_pe, cmp_w1, cmp_b1, cmp_w2, nsa_w_out, w_o, norm2_g,
           ffn_w13, ffn_w2):
    b, s, d = x.shape
    depth = w_in.shape[0]
    t = b * s
    cos128, sin128 = _rope_tables(s)
    hid = jnp.arange(128) // HEAD_DIM
    mblk = jnp.where(hid[:, None] == hid[None, :], 1.0 / HEAD_DIM, 0.0).astype(BF16)

    w_in_segs = (w_in[:, :, :GATE_COL].astype(BF16),
                 jnp.pad(w_in[:, :, GATE_COL:GATE_COL + N_GATE],
                         ((0, 0), (0, 0), (0, GATE_PAD - N_GATE))).astype(BF16),
                 w_in[:, :, GATE_COL + N_GATE:].astype(BF16))
    a_w_out_b = a_w_out.astype(BF16)
    b_w_out_b = b_w_out.astype(BF16)
    cmp_w1_b = cmp_w1.astype(BF16)
    cmp_w2_b = jnp.pad(cmp_w2, ((0, 0), (0, 0), (0, 0), (0, 128 - HEAD_DIM))).astype(BF16)
    nsa_w_out_b = nsa_w_out.astype(BF16)
    w_o_b = w_o.astype(BF16)
    ffn_w13_b = ffn_w13.astype(BF16)
    ffn_w2_b = ffn_w2.astype(BF16)
    qg128 = jnp.concatenate([q_norm_g, q_norm_g], axis=-1)[:, None, :]
    kg128 = jnp.concatenate([k_norm_g, k_norm_g], axis=-1)
    pe2 = cmp_pe.reshape(depth, 2, 2, (CMP_BLOCK // 2) * HEAD_DIM)
    b1 = cmp_b1[:, :, None, :]

    xf = x.reshape(t, d)
    h = _rmsnorm(xf, norm1_g[0:1])
    for l in range(depth):
        proj = _in_proj(h, w_in_segs, l)
        proj3 = proj.reshape(b, s, PROJ_WIDTH)
        mab = _conv_mixers(proj3, a_conv_w[l], b_conv_w[l], b_conv_b[l:l + 1], b_ln_g[l:l + 1],
                           b_ln_b[l:l + 1], a_w_out_b, b_w_out_b, l)
        qt, kcmp, vcmp, ka, vt, gates_t = _attn_prep(proj3, cos128, sin128, qg128[l], kg128[l], mblk)
        nb = s // CMP_STRIDE
        kc, vct = _compress(kcmp.reshape(b, N_KV, nb, CMP_STRIDE * HEAD_DIM),
                            vcmp.reshape(b, N_KV, nb, CMP_STRIDE * HEAD_DIM),
                            pe2[l], cmp_w1_b, l, b1[l], cmp_w2_b[l])
        o = _attention(qt, kc, vct, ka, vt, gates_t)
        xf, h2 = _merge(o.reshape(t, ATTN_WIDTH), mab.reshape(t, d), proj, xf,
                        nsa_w_out_b, w_o_b, l, norm2_g[l:l + 1])
        g_next = norm1_g[l + 1:l + 2] if l + 1 < depth else norm1_g[l:l + 1]
        xf, h = _ffn(h2, xf, ffn_w13_b, ffn_w2_b, l, g_next)
    return xf.reshape(b, s, d)
```

```python
import functools

import jax
import jax.numpy as jnp
from jax import lax
from jax.experimental import pallas as pl
from jax.experimental.pallas import tpu as pltpu

F32 = jnp.float32
BF16 = jnp.bfloat16

D_MODEL = 1024
DEPTH = 4
CONV_A_WIDTH = 512
CONV_A_K = 3
CONV_B_WIDTH = 512
CONV_B_K = 31
N_HEADS = 8
N_KV = 2
HEAD_DIM = 64
GROUP = N_HEADS // N_KV
ATTN_WIDTH = N_HEADS * HEAD_DIM
KV_WIDTH = N_KV * HEAD_DIM
N_BRANCH = 3
CMP_BLOCK = 32
CMP_STRIDE = 16
CMP_HIDDEN = 256
SLC_BLOCK = 64
SLC_TOPN = 16
WINDOW = 512
ROPE_THETA = 10000.0
D_FF = -(-8 * D_MODEL // (3 * 256)) * 256
EPS = 1e-6
NEG = -1e30
SCALE = HEAD_DIM ** -0.5
QSCALE = SCALE * 1.4426950408889634

N_GATE = N_HEADS * N_BRANCH
GATE_COL = 3 * CONV_A_WIDTH + 2 * CONV_B_WIDTH + ATTN_WIDTH + 6 * KV_WIDTH
GATE_PAD = 256
PROJ_WIDTH = GATE_COL + GATE_PAD + 3 * D_MODEL

MXU_COLS = 256
SUBLANES = 8
VMEM_LIMIT = 56 * 1024 * 1024
HALO = 32
CONV_ROWS = 32
ATTN_CHUNK = 256
GATE_ROWS = 16
VT_ROWS = HEAD_DIM + 16


def _cparams(*sem):
    return pltpu.CompilerParams(dimension_semantics=sem, vmem_limit_bytes=VMEM_LIMIT)


def _rmsnorm_kernel(x_ref, g_ref, h_ref):
    x = x_ref[...]
    ms = jnp.mean(x * x, axis=-1, keepdims=True)
    h_ref[...] = (x * lax.rsqrt(ms + EPS) * g_ref[...]).astype(h_ref.dtype)


def _rmsnorm(x, g, tm=1024):
    t, d = x.shape
    return pl.pallas_call(
        _rmsnorm_kernel,
        out_shape=jax.ShapeDtypeStruct((t, d), BF16),
        grid=(t // tm,),
        in_specs=[pl.BlockSpec((tm, d), lambda i: (i, 0)), pl.BlockSpec((1, d), lambda i: (0, 0))],
        out_specs=pl.BlockSpec((tm, d), lambda i: (i, 0)),
        compiler_params=_cparams("parallel"),
        name="rmsnorm",
    )(x, g)


def _layer_spec(arr, l, resident=False):
    zeros = (0,) * (arr.ndim - 1)
    return pl.BlockSpec((1,) + arr.shape[1:], lambda *_: (l,) + zeros,
                        pipeline_mode=pl.Buffered(1) if resident else None)


def _in_proj_kernel(h_ref, *refs, chunk):
    w_refs, o_ref = refs[:-1], refs[-1]
    h = h_ref[...]
    col = 0
    for w_ref in w_refs:
        n = w_ref.shape[2]
        for lo in range(0, n, chunk):
            hi = min(lo + chunk, n)
            o_ref[:, col + lo:col + hi] = jnp.dot(
                h, w_ref[0, :, lo:hi], preferred_element_type=F32).astype(o_ref.dtype)
        col += n


def _in_proj(h, w_segments, l, tm=512, chunk=1280):
    m, k = h.shape
    n = sum(w.shape[2] for w in w_segments)
    return pl.pallas_call(
        functools.partial(_in_proj_kernel, chunk=chunk),
        out_shape=jax.ShapeDtypeStruct((m, n), BF16),
        grid=(m // tm,),
        in_specs=[pl.BlockSpec((tm, k), lambda i: (i, 0))] + [_layer_spec(w, l, resident=True) for w in w_segments],
        out_specs=pl.BlockSpec((tm, n), lambda i: (i, 0)),
        compiler_params=_cparams("parallel"),
        name="in_proj",
    )(h, *w_segments)


def _conv_kernel(ac_ref, ab_ref, ah_ref, ba_ref, bg_ref, ga_ref, gb_ref,
                 wa_ref, wb_ref, bb_ref, lng_ref, lnb_ref, wao_ref, wbo_ref,
                 out_ref, ua_ext, ub_ext, ub_sh, vb_scr):
    ts = out_ref.shape[1]
    width = ua_ext.shape[1]

    @pl.when(pl.program_id(1) == 0)
    def _():
        ua_ext[0:HALO, :] = jnp.zeros((HALO, width), F32)
        ub_ext[0:HALO, :] = jnp.zeros((HALO, width), F32)

    ua_ext[HALO:HALO + ts, :] = ac_ref[0].astype(F32) * ah_ref[0].astype(F32)
    ub_ext[HALO:HALO + ts, :] = ba_ref[0].astype(F32) * jax.nn.sigmoid(bg_ref[0].astype(F32))

    wa = wa_ref[...]
    va = wa[0:1] * ua_ext[HALO - 2:HALO - 2 + ts, :]
    va = va + wa[1:2] * ua_ext[HALO - 1:HALO - 1 + ts, :]
    va = va + wa[2:3] * ua_ext[HALO:HALO + ts, :]
    ya = jnp.dot((ab_ref[0].astype(F32) * va).astype(BF16), wao_ref[0], preferred_element_type=F32)

    span = HALO + ts - SUBLANES
    for j in range(1, SUBLANES):
        ub_sh[j - 1, 0:span, :] = ub_ext[j:j + span, :]
    wb = wb_ref[...]
    bias = jnp.broadcast_to(bb_ref[...], (CONV_ROWS, width))
    for c in range(ts // CONV_ROWS):
        acc = bias
        for k in range(CONV_B_K):
            off = HALO - (CONV_B_K - 1) + k
            j = off % SUBLANES
            r0 = off - j + c * CONV_ROWS
            src = ub_ext if j == 0 else ub_sh.at[j - 1]
            acc = acc + wb[k:k + 1] * src[r0:r0 + CONV_ROWS, :]
        vb_scr[c * CONV_ROWS:(c + 1) * CONV_ROWS, :] = acc
    u = vb_scr[...]
    mu = jnp.mean(u, axis=-1, keepdims=True)
    var = jnp.mean(jnp.square(u - mu), axis=-1, keepdims=True)
    y = (u - mu) * lax.rsqrt(var + EPS)
    y = y * lng_ref[...] + lnb_ref[...]
    y = y * jax.nn.sigmoid(y)
    yb = jnp.dot(y.astype(BF16), wbo_ref[0], preferred_element_type=F32)

    out_ref[0] = (jax.nn.sigmoid(ga_ref[0].astype(F32)) * ya
                  + jax.nn.sigmoid(gb_ref[0].astype(F32)) * yb).astype(out_ref.dtype)

    ua_ext[0:HALO, :] = ua_ext[ts:ts + HALO, :]
    ub_ext[0:HALO, :] = ub_ext[ts:ts + HALO, :]


def _conv_mixers(proj, wa, wb, bb, lng, lnb, wao, wbo, l, ts=512):
    b, s, _ = proj.shape
    w = CONV_A_WIDTH
    gate_blk = (GATE_COL + GATE_PAD) // D_MODEL

    def col(width, idx):
        return pl.BlockSpec((1, ts, width), lambda bi, si: (bi, si, idx))

    def full(arr):
        return pl.BlockSpec(arr.shape, lambda bi, si: (0,) * arr.ndim)

    return pl.pallas_call(
        _conv_kernel,
        out_shape=jax.ShapeDtypeStruct((b, s, D_MODEL), BF16),
        grid=(b, s // ts),
        in_specs=[col(w, 0), col(w, 1), col(w, 2), col(w, 3), col(w, 4),
                  col(D_MODEL, gate_blk), col(D_MODEL, gate_blk + 1),
                  full(wa), full(wb), full(bb), full(lng), full(lnb), _layer_spec(wao, l), _layer_spec(wbo, l)],
        out_specs=pl.BlockSpec((1, ts, D_MODEL), lambda bi, si: (bi, si, 0)),
        scratch_shapes=[pltpu.VMEM((HALO + ts, w), F32), pltpu.VMEM((HALO + ts, w), F32),
                        pltpu.VMEM((SUBLANES - 1, HALO + ts - SUBLANES, w), F32),
                        pltpu.VMEM((ts, w), F32)],
        compiler_params=_cparams("parallel", "arbitrary"),
        name="conv_mixers",
    )(proj, proj, proj, proj, proj, proj, proj, wa, wb, bb, lng, lnb, wao, wbo)


def _prep_kernel(q_ref, kv_ref, gt_ref, cos_ref, sin_ref, qg_ref, kg_ref, mblk_ref,
                 qt_ref, kcmp_ref, vcmp_ref, ka_ref, vt_ref, go_ref):
    ts = q_ref.shape[1]
    ck = vt_ref.shape[5]
    cos = cos_ref[...]
    sin = sin_ref[...]
    mblk = mblk_ref[...]
    lane = lax.broadcasted_iota(jnp.int32, (ts, 128), 1)
    first_half = (lane & (HEAD_DIM - 1)) < HEAD_DIM // 2
    low = lane < HEAD_DIM
    tabs = pl.program_id(1) * ts + lax.broadcasted_iota(jnp.int32, (ts, 128), 0)
    blk_onehot = jnp.where(lane - HEAD_DIM == tabs // SLC_BLOCK, 1.0, 0.0)
    ones_rows = jnp.where(lax.broadcasted_iota(jnp.int32, (VT_ROWS - HEAD_DIM, ck), 0) == 0, 1.0, 0.0).astype(BF16)

    def per_group(x):
        return x, pltpu.roll(x, HEAD_DIM, 1)

    def norm_rope(x, g):
        xx = x * x
        hi = xx.astype(BF16)
        lo = (xx - hi.astype(F32)).astype(BF16)
        ms = (jnp.dot(hi, mblk, preferred_element_type=F32)
              + jnp.dot(lo, mblk, preferred_element_type=F32))
        y = x * lax.rsqrt(ms + EPS) * g
        partner = jnp.where(first_half, pltpu.roll(y, 128 - HEAD_DIM // 2, 1), pltpu.roll(y, HEAD_DIM // 2, 1))
        return y * cos + partner * sin

    qg = qg_ref[...]
    for j in range(ATTN_WIDTH // 128):
        o_t = (norm_rope(q_ref[0, :, 128 * j:128 * (j + 1)].astype(F32), qg) * QSCALE).T
        qt_ref[0, 2 * j] = o_t[:HEAD_DIM].astype(BF16)
        qt_ref[0, 2 * j + 1] = o_t[HEAD_DIM:].astype(BF16)
    for i in range(3):
        k = norm_rope(kv_ref[0, :, 256 * i:256 * i + 128].astype(F32), kg_ref[i:i + 1, :])
        v = kv_ref[0, :, 256 * i + 128:256 * i + 256].astype(F32)
        if i == 0:
            for g, (kg, vg) in enumerate(zip(per_group(k), per_group(v))):
                kcmp_ref[0, g] = kg[:, :HEAD_DIM].astype(BF16)
                vcmp_ref[0, g] = vg[:, :HEAD_DIM].astype(BF16)
        else:
            extra = blk_onehot if i == 1 else 0.0
            for g, kg in enumerate(per_group(k)):
                ka_ref[0, i - 1, g] = jnp.where(low, kg, extra).astype(BF16)
            v_t = v.T
            for g in range(N_KV):
                for c in range(ts // ck):
                    vt_ref[0, i - 1, g, c, :HEAD_DIM, :] = (
                        v_t[g * HEAD_DIM:(g + 1) * HEAD_DIM, c * ck:(c + 1) * ck].astype(BF16))
                    vt_ref[0, i - 1, g, c, HEAD_DIM:, :] = ones_rows
    sg = jax.nn.sigmoid(gt_ref[0].astype(F32))
    ngate = go_ref.shape[2]
    go_ref[0, 0] = sg.T[:ngate]
    go_ref[0, 1] = pltpu.roll(sg, 128 - GROUP * N_BRANCH, 1).T[:ngate]


def _attn_prep(proj, cos128, sin128, qg128, kg128, mblk, ts=512, ck=ATTN_CHUNK):
    b, s, _ = proj.shape

    def full(arr):
        return pl.BlockSpec(arr.shape, lambda bi, si: (0,) * arr.ndim)

    q_blk = (3 * CONV_A_WIDTH + 2 * CONV_B_WIDTH) // ATTN_WIDTH
    kv_blk = (q_blk + 1) * ATTN_WIDTH // (6 * KV_WIDTH)
    gt_blk = GATE_COL // 128
    return pl.pallas_call(
        _prep_kernel,
        out_shape=(jax.ShapeDtypeStruct((b, N_HEADS, HEAD_DIM, s), BF16),
                   jax.ShapeDtypeStruct((b, N_KV, s, HEAD_DIM), BF16),
                   jax.ShapeDtypeStruct((b, N_KV, s, HEAD_DIM), BF16),
                   jax.ShapeDtypeStruct((b, 2, N_KV, s, 128), BF16),
                   jax.ShapeDtypeStruct((b, 2, N_KV, s // ck, VT_ROWS, ck), BF16),
                   jax.ShapeDtypeStruct((b, N_KV, GATE_ROWS, s), F32)),
        grid=(b, s // ts),
        in_specs=[pl.BlockSpec((1, ts, ATTN_WIDTH), lambda bi, si: (bi, si, q_blk)),
                  pl.BlockSpec((1, ts, 6 * KV_WIDTH), lambda bi, si: (bi, si, kv_blk)),
                  pl.BlockSpec((1, ts, 128), lambda bi, si: (bi, si, gt_blk)),
                  pl.BlockSpec((ts, 128), lambda bi, si: (si, 0)),
                  pl.BlockSpec((ts, 128), lambda bi, si: (si, 0)),
                  full(qg128), full(kg128), full(mblk)],
        out_specs=(pl.BlockSpec((1, N_HEADS, HEAD_DIM, ts), lambda bi, si: (bi, 0, 0, si)),
                   pl.BlockSpec((1, N_KV, ts, HEAD_DIM), lambda bi, si: (bi, 0, si, 0)),
                   pl.BlockSpec((1, N_KV, ts, HEAD_DIM), lambda bi, si: (bi, 0, si, 0)),
                   pl.BlockSpec((1, 2, N_KV, ts, 128), lambda bi, si: (bi, 0, 0, si, 0)),
                   pl.BlockSpec((1, 2, N_KV, ts // ck, VT_ROWS, ck), lambda bi, si: (bi, 0, 0, si, 0, 0)),
                   pl.BlockSpec((1, N_KV, GATE_ROWS, ts), lambda bi, si: (bi, 0, 0, si))),
        compiler_params=_cparams("parallel", "parallel"),
        name="attn_prep",
    )(proj, proj, proj, cos128, sin128, qg128, kg128, mblk)


def _compress_kernel(k_ref, v_ref, pe_ref, w1_ref, b1_ref, w2_ref, kc_ref, vc_ref):
    nb = k_ref.shape[2]
    half = w1_ref.shape[2] // 2
    row = lax.broadcasted_iota(jnp.int32, (nb, w2_ref.shape[2]), 0)
    for i, (src, dst) in enumerate(((k_ref, kc_ref), (v_ref, vc_ref))):
        u = src[0, 0].astype(F32)
        top = (u + pe_ref[i, 0:1, :]).astype(BF16)
        bot = (u + pe_ref[i, 1:2, :]).astype(BF16)
        a = jnp.dot(top, w1_ref[0, i, :half, :], preferred_element_type=F32)
        bm = jnp.dot(bot, w1_ref[0, i, half:, :], preferred_element_type=F32)
        hid = a + pltpu.roll(bm, nb - 1, 0) + b1_ref[i]
        hid = jax.nn.gelu(hid)
        out = jnp.dot(hid.astype(BF16), w2_ref[i], preferred_element_type=F32)
        out = jnp.where(row < nb - 1, out, 0.0)
        dst[0, 0] = (out if i == 0 else out.T).astype(dst.dtype)


def _compress(k4, v4, pe, w1, l, b1, w2):
    b, _, nb, _ = k4.shape
    width = w2.shape[2]
    assert nb == width

    def full(arr):
        return pl.BlockSpec(arr.shape, lambda bi, gi: (0,) * arr.ndim)

    blk = pl.BlockSpec((1, 1, nb, CMP_STRIDE * HEAD_DIM), lambda bi, gi: (bi, gi, 0, 0))
    out_blk = pl.BlockSpec((1, 1, nb, width), lambda bi, gi: (bi, gi, 0, 0))
    return pl.pallas_call(
        _compress_kernel,
        out_shape=(jax.ShapeDtypeStruct((b, N_KV, nb, width), BF16),) * 2,
        grid=(b, N_KV),
        in_specs=[blk, blk, full(pe), _layer_spec(w1, l), full(b1), full(w2)],
        out_specs=(out_blk, out_blk),
        compiler_params=_cparams("parallel", "parallel"),
        name="compress",
    )(k4, v4, pe, w1, b1, w2)


def _attn_kernel(q_ref, kc_ref, vct_ref, ks_ref, vs_ref, kw_ref, vw_ref, gt_ref, o_ref,
                 qa_scr, s_scr, m_scr, acc_scr, oc_scr):
    tq = q_ref.shape[3]
    ck = vs_ref.shape[5]
    s_len = ks_ref.shape[3]
    nc = kc_ref.shape[2]
    n_sel = s_len // SLC_BLOCK
    cols = GROUP * tq
    qi = pl.program_id(2)
    t0 = qi * tq

    def head(r):
        return slice(r * tq, (r + 1) * tq)

    def keys(k_ref, chunk):
        return k_ref[0, 0, 0, pl.ds(pl.multiple_of(chunk * ck, ck), ck), :]

    def gate(branch):
        gt = gt_ref[0, 0]
        return jnp.concatenate([gt[N_BRANCH * r + branch:N_BRANCH * r + branch + 1] for r in range(GROUP)], axis=1)

    q0 = jnp.concatenate([jnp.concatenate([q_ref[0, r] for r in range(GROUP)], axis=1),
                          jnp.zeros((128 - HEAD_DIM, cols), BF16)], axis=0)
    qa_scr[...] = q0

    kj = lax.broadcasted_iota(jnp.int32, (ck, tq), 0)
    qc = lax.broadcasted_iota(jnp.int32, (ck, tq), 1)
    causal = jnp.where(kj <= qc, 0.0, NEG)
    tail = jnp.where(kj > qc, 0.0, NEG)

    def produce(k_ref, chunk, qmat, slot):
        s_scr[slot] = jnp.dot(keys(k_ref, chunk), qmat, preferred_element_type=F32)

    def consume(slot, v_ref, chunk, mask):
        s = s_scr[slot]
        if mask is not None:
            s = s + jnp.tile(mask, (1, GROUP))
        m_prev = m_scr[...]
        m_new = jnp.maximum(m_prev, jnp.max(s, axis=0, keepdims=True))
        alpha = jnp.exp2(m_prev - m_new)
        p = jnp.exp2(s - m_new).astype(BF16)
        acc_scr[...] = alpha * acc_scr[...] + jnp.dot(v_ref[0, 0, 0, chunk], p, preferred_element_type=F32)
        m_scr[...] = m_new

    def reset():
        m_scr[...] = jnp.full(m_scr.shape, NEG, F32)
        acc_scr[...] = jnp.zeros(acc_scr.shape, F32)

    def finish():
        acc = acc_scr[...]
        return acc[:HEAD_DIM] / acc[HEAD_DIM:HEAD_DIM + 1]

    n_win = WINDOW // ck
    win_chunks = [(qi, causal)]
    for d in range(1, n_win + 1):
        base = tail if d == n_win else jnp.zeros((ck, tq), F32)
        win_chunks.append((jnp.maximum(qi - d, 0), jnp.where(qi >= d, base, NEG)))
    win_slot0 = 2
    for i, (c, _) in enumerate(win_chunks):
        produce(kw_ref, c, q0, win_slot0 + i)

    tpos = t0 + (lax.broadcasted_iota(jnp.int32, (nc, cols), 1) & (tq - 1))
    cidx = lax.broadcasted_iota(jnp.int32, (nc, cols), 0)
    cmask = ((cidx * CMP_STRIDE + (CMP_BLOCK - 1)) <= tpos) & (cidx < nc - 1)
    sc = jnp.dot(kc_ref[0, 0], q0, preferred_element_type=F32)
    sc = jnp.where(cmask, sc, NEG)
    mc = jnp.max(sc, axis=0, keepdims=True)
    ec = jnp.where(cmask, jnp.exp2(sc - mc), 0.0)
    dc = jnp.sum(ec, axis=0, keepdims=True)
    pc = ec / jnp.where(dc > 0.0, dc, 1.0)
    o_cmp = jnp.dot(vct_ref[0, 0], pc.astype(BF16), preferred_element_type=F32)

    psum = pc[:, head(0)] + pc[:, head(1)] + pc[:, head(2)] + pc[:, head(3)]
    p_hi = psum.astype(BF16)
    p_lo = (psum - p_hi.astype(F32)).astype(BF16)
    jrow = lax.broadcasted_iota(jnp.int32, (n_sel, nc), 0)
    ccol = lax.broadcasted_iota(jnp.int32, (n_sel, nc), 1)
    ovl = ((ccol * CMP_STRIDE < jrow * SLC_BLOCK + SLC_BLOCK)
           & (ccol * CMP_STRIDE + CMP_BLOCK > jrow * SLC_BLOCK))
    ovl = jnp.where(ovl, 1.0, 0.0).astype(BF16)
    imp = (jnp.dot(ovl, p_hi, preferred_element_type=F32)
           + jnp.dot(ovl, p_lo, preferred_element_type=F32))

    jj = lax.broadcasted_iota(jnp.int32, (n_sel, tq), 0)
    cur = (t0 + lax.broadcasted_iota(jnp.int32, (n_sel, tq), 1)) // SLC_BLOCK
    valid = jj <= cur
    forced = (jj == 0) | (jj == cur) | (jj == cur - 1)
    score = jnp.where(forced, jnp.inf, jnp.where(valid, imp, -jnp.inf))
    rank = jnp.zeros((n_sel, tq), F32)
    for jp in range(n_sel):
        other = score[jp:jp + 1, :]
        ahead = (other > score) | ((other == score) & (jj > jp))
        rank = rank + jnp.where(ahead, 1.0, 0.0)
    sel = jnp.where(valid & (rank < float(min(SLC_TOPN, n_sel))), 1.0, 0.0)

    pen = jnp.where(sel > 0.5, 0.0, NEG).astype(BF16)
    for r in range(GROUP):
        qa_scr[HEAD_DIM:HEAD_DIM + n_sel, head(r)] = pen

    produce(ks_ref, 0, qa_scr[...], 0)

    reset()
    for i, (c, msk) in enumerate(win_chunks):
        consume(win_slot0 + i, vw_ref, c, msk)
    oc_scr[...] = gate(0) * o_cmp[:HEAD_DIM] + gate(2) * finish()

    reset()

    def pair_body(j, carry):
        produce(ks_ref, 2 * j + 1, qa_scr[...], 1)
        consume(0, vs_ref, 2 * j, None)
        produce(ks_ref, 2 * j + 2, qa_scr[...], 0)
        consume(1, vs_ref, 2 * j + 1, None)
        return carry

    lax.fori_loop(0, qi // 2, pair_body, 0)

    @pl.when(qi % 2 == 1)
    def _():
        produce(ks_ref, qi, qa_scr[...], 1)
        consume(0, vs_ref, qi - 1, None)
        consume(1, vs_ref, qi, causal)

    @pl.when(qi % 2 == 0)
    def _():
        consume(0, vs_ref, qi, causal)

    o_slc = finish()

    o_all = oc_scr[...] + gate(1) * o_slc
    o_ref[0] = jnp.concatenate([o_all[:, head(r)] for r in range(GROUP)], axis=0).T.astype(o_ref.dtype)


def _attention(qt, kc, vct, ka, vt, gates_t, tq=ATTN_CHUNK):
    b, _, _, s = qt.shape
    nc = kc.shape[2]
    ck = vt.shape[5]
    cols = GROUP * tq
    assert tq == ck and WINDOW % ck == 0 and tq & (tq - 1) == 0

    kseq = lambda branch: pl.BlockSpec((1, 1, 1, s, 128), lambda bi, gi, qi: (bi, branch, gi, 0, 0))
    vseq = lambda branch: pl.BlockSpec((1, 1, 1, s // ck, VT_ROWS, ck), lambda bi, gi, qi: (bi, branch, gi, 0, 0, 0))
    cblk = pl.BlockSpec((1, 1, nc, 128), lambda bi, gi, qi: (bi, gi, 0, 0))
    return pl.pallas_call(
        _attn_kernel,
        out_shape=jax.ShapeDtypeStruct((b, s, ATTN_WIDTH), BF16),
        grid=(b, N_KV, s // tq),
        in_specs=[pl.BlockSpec((1, GROUP, HEAD_DIM, tq), lambda bi, gi, qi: (bi, gi, 0, qi)),
                  cblk, cblk, kseq(0), vseq(0), kseq(1), vseq(1),
                  pl.BlockSpec((1, 1, GATE_ROWS, tq), lambda bi, gi, qi: (bi, gi, 0, qi))],
        out_specs=pl.BlockSpec((1, tq, GROUP * HEAD_DIM), lambda bi, gi, qi: (bi, qi, gi)),
        scratch_shapes=[pltpu.VMEM((128, cols), BF16),
                        pltpu.VMEM((WINDOW // ck + 3, ck, cols), F32),
                        pltpu.VMEM((1, cols), F32),
                        pltpu.VMEM((VT_ROWS, cols), F32),
                        pltpu.VMEM((HEAD_DIM, cols), F32)],
        compiler_params=_cparams("parallel", "parallel", "arbitrary"),
        name="nsa_attention",
    )(qt, kc, vct, ka, vt, ka, vt, gates_t)


def _rms(x, g):
    ms = jnp.mean(x * x, axis=-1, keepdims=True)
    return x * lax.rsqrt(ms + EPS) * g


def _merge_ffn_kernel(o_ref, mab_ref, gc_ref, x_ref, wn_ref, wo_ref, g2_ref, w13_ref, w2_ref, gn_ref,
                      xo_ref, hn_ref, *, bounds):
    yc = jnp.dot(o_ref[...], wn_ref[0], preferred_element_type=F32)
    mixed = mab_ref[...].astype(F32) + jax.nn.sigmoid(gc_ref[...].astype(F32)) * yc
    x = x_ref[...] + jnp.dot(mixed.astype(BF16), wo_ref[0], preferred_element_type=F32)
    h = _rms(x, g2_ref[...]).astype(BF16)
    for lo, hi in bounds:
        u = jnp.dot(h, w13_ref[0, :, lo:hi], preferred_element_type=F32)
        v = jnp.dot(h, w13_ref[0, :, D_FF + lo:D_FF + hi], preferred_element_type=F32)
        act = (u * jax.nn.sigmoid(u) * v).astype(BF16)
        x = x + jnp.dot(act, w2_ref[0, lo:hi, :], preferred_element_type=F32)
    xo_ref[...] = x
    hn_ref[...] = _rms(x, gn_ref[...]).astype(hn_ref.dtype)


def _merge_ffn(o, mab, proj2d, x, wn, wo, g2, w13, w2, l, g_next, tm=512, n_chunks=2):
    t, d = x.shape
    gc_blk = (GATE_COL + GATE_PAD) // D_MODEL + 2
    tiles = D_FF // MXU_COLS
    edges = [MXU_COLS * (tiles * i // n_chunks) for i in range(n_chunks + 1)]
    row = lambda width, col=0: pl.BlockSpec((tm, width), lambda i: (i, col))
    vec = pl.BlockSpec((1, d), lambda i: (0, 0))
    return pl.pallas_call(
        functools.partial(_merge_ffn_kernel, bounds=tuple(zip(edges[:-1], edges[1:]))),
        out_shape=(jax.ShapeDtypeStruct((t, d), F32), jax.ShapeDtypeStruct((t, d), BF16)),
        grid=(t // tm,),
        in_specs=[row(ATTN_WIDTH), row(d), row(d, gc_blk), row(d),
                  _layer_spec(wn, l, resident=True), _layer_spec(wo, l, resident=True), vec,
                  _layer_spec(w13, l, resident=True), _layer_spec(w2, l, resident=True), vec],
        out_specs=(row(d), row(d)),
        compiler_params=_cparams("parallel"),
        name="merge_ffn",
    )(o, mab, proj2d, x, wn, wo, g2, w13, w2, g_next)


def _rope_tables(s):
    pos = jnp.arange(s, dtype=F32)
    inv = 1.0 / (ROPE_THETA ** (jnp.arange(0, HEAD_DIM, 2, dtype=F32) / HEAD_DIM))
    ang = pos[:, None] * inv[None, :]
    cos, sin = jnp.cos(ang), jnp.sin(ang)
    cos128 = jnp.concatenate([cos, cos, cos, cos], axis=-1)
    sin128 = jnp.concatenate([-sin, sin, -sin, sin], axis=-1)
    return cos128, sin128


def kernel(x, norm1_g, w_in, a_conv_w, a_w_out, b_conv_w, b_conv_b, b_ln_g, b_ln_b, b_w_out,
           q_norm_g, k_norm_g, cmp_pe, cmp_w1, cmp_b1, cmp_w2, nsa_w_out, w_o, norm2_g,
           ffn_w13, ffn_w2):
    b, s, d = x.shape
    depth = w_in.shape[0]
    t = b * s
    cos128, sin128 = _rope_tables(s)
    hid = jnp.arange(128) // HEAD_DIM
    mblk = jnp.where(hid[:, None] == hid[None, :], 1.0 / HEAD_DIM, 0.0).astype(BF16)

    w_in_segs = (w_in[:, :, :GATE_COL].astype(BF16),
                 jnp.pad(w_in[:, :, GATE_COL:GATE_COL + N_GATE],
                         ((0, 0), (0, 0), (0, GATE_PAD - N_GATE))).astype(BF16),
                 w_in[:, :, GATE_COL + N_GATE:].astype(BF16))
    a_w_out_b = a_w_out.astype(BF16)
    b_w_out_b = b_w_out.astype(BF16)
    cmp_w1_b = cmp_w1.astype(BF16)
    cmp_w2_b = jnp.pad(cmp_w2, ((0, 0), (0, 0), (0, 0), (0, 128 - HEAD_DIM))).astype(BF16)
    nsa_w_out_b = nsa_w_out.astype(BF16)
    w_o_b = w_o.astype(BF16)
    ffn_w13_b = ffn_w13.astype(BF16)
    ffn_w2_b = ffn_w2.astype(BF16)
    qg128 = jnp.concatenate([q_norm_g, q_norm_g], axis=-1)[:, None, :]
    kg128 = jnp.concatenate([k_norm_g, k_norm_g], axis=-1)
    pe2 = cmp_pe.reshape(depth, 2, 2, (CMP_BLOCK // 2) * HEAD_DIM)
    b1 = cmp_b1[:, :, None, :]

    xf = x.reshape(t, d)
    h = _rmsnorm(xf, norm1_g[0:1])
    for l in range(depth):
        proj = _in_proj(h, w_in_segs, l)
        proj3 = proj.reshape(b, s, PROJ_WIDTH)
        mab = _conv_mixers(proj3, a_conv_w[l], b_conv_w[l], b_conv_b[l:l + 1], b_ln_g[l:l + 1],
                           b_ln_b[l:l + 1], a_w_out_b, b_w_out_b, l)
        qt, kcmp, vcmp, ka, vt, gates_t = _attn_prep(proj3, cos128, sin128, qg128[l], kg128[l], mblk)
        nb = s // CMP_STRIDE
        kc, vct = _compress(kcmp.reshape(b, N_KV, nb, CMP_STRIDE * HEAD_DIM),
                            vcmp.reshape(b, N_KV, nb, CMP_STRIDE * HEAD_DIM),
                            pe2[l], cmp_w1_b, l, b1[l], cmp_w2_b[l])
        o = _attention(qt, kc, vct, ka, vt, gates_t)
        g_next = norm1_g[l + 1:l + 2] if l + 1 < depth else norm1_g[l:l + 1]
        xf, h = _merge_ffn(o.reshape(t, ATTN_WIDTH), mab.reshape(t, d), proj, xf, nsa_w_out_b, w_o_b,
                           norm2_g[l:l + 1], ffn_w13_b, ffn_w2_b, l, g_next)
    return xf.reshape(b, s, d)
```

```python
import functools

import jax
import jax.numpy as jnp
from jax import lax
from jax.experimental import pallas as pl
from jax.experimental.pallas import tpu as pltpu

F32 = jnp.float32
BF16 = jnp.bfloat16

D_MODEL = 1024
DEPTH = 4
CONV_A_WIDTH = 512
CONV_A_K = 3
CONV_B_WIDTH = 512
CONV_B_K = 31
N_HEADS = 8
N_KV = 2
HEAD_DIM = 64
GROUP = N_HEADS // N_KV
ATTN_WIDTH = N_HEADS * HEAD_DIM
KV_WIDTH = N_KV * HEAD_DIM
N_BRANCH = 3
CMP_BLOCK = 32
CMP_STRIDE = 16
CMP_HIDDEN = 256
SLC_BLOCK = 64
SLC_TOPN = 16
WINDOW = 512
ROPE_THETA = 10000.0
D_FF = -(-8 * D_MODEL // (3 * 256)) * 256
EPS = 1e-6
NEG = -1e30
SCALE = HEAD_DIM ** -0.5
QSCALE = SCALE * 1.4426950408889634

N_GATE = N_HEADS * N_BRANCH
GATE_COL = 3 * CONV_A_WIDTH + 2 * CONV_B_WIDTH + ATTN_WIDTH + 6 * KV_WIDTH
GATE_PAD = 256
PROJ_WIDTH = GATE_COL + GATE_PAD + 3 * D_MODEL

MXU_COLS = 256
SUBLANES = 8
VMEM_LIMIT = 56 * 1024 * 1024
HALO = 32
CONV_ROWS = 32
ATTN_CHUNK = 256
GATE_ROWS = 16
VT_ROWS = HEAD_DIM + 16


def _cparams(*sem):
    return pltpu.CompilerParams(dimension_semantics=sem, vmem_limit_bytes=VMEM_LIMIT)


def _rmsnorm_kernel(x_ref, g_ref, h_ref):
    x = x_ref[...]
    ms = jnp.mean(x * x, axis=-1, keepdims=True)
    h_ref[...] = (x * lax.rsqrt(ms + EPS) * g_ref[...]).astype(h_ref.dtype)


def _rmsnorm(x, g, tm=1024):
    t, d = x.shape
    return pl.pallas_call(
        _rmsnorm_kernel,
        out_shape=jax.ShapeDtypeStruct((t, d), BF16),
        grid=(t // tm,),
        in_specs=[pl.BlockSpec((tm, d), lambda i: (i, 0)), pl.BlockSpec((1, d), lambda i: (0, 0))],
        out_specs=pl.BlockSpec((tm, d), lambda i: (i, 0)),
        compiler_params=_cparams("parallel"),
        name="rmsnorm",
    )(x, g)


def _layer_spec(arr, l, resident=False):
    zeros = (0,) * (arr.ndim - 1)
    return pl.BlockSpec((1,) + arr.shape[1:], lambda *_: (l,) + zeros,
                        pipeline_mode=pl.Buffered(1) if resident else None)


def _in_proj_kernel(h_ref, wm_ref, wt_ref, o_ref, wg_scr, *, chunk):
    n_g = wg_scr.shape[1]

    @pl.when(pl.program_id(0) == 0)
    def _():
        wg_scr[...] = wt_ref[0, :, N_GATE:N_GATE + n_g]

    h = h_ref[...]

    def project(w_of, n, col):
        for lo in range(0, n, chunk):
            hi = min(lo + chunk, n)
            o_ref[:, col + lo:col + hi] = jnp.dot(h, w_of(lo, hi), preferred_element_type=F32).astype(o_ref.dtype)

    project(lambda lo, hi: wm_ref[0, :, lo:hi], GATE_COL, 0)
    project(lambda lo, hi: wt_ref[0, :, lo:hi], GATE_PAD, GATE_COL)
    project(lambda lo, hi: wg_scr[:, lo:hi], n_g, GATE_COL + GATE_PAD)


def _in_proj(h, w_in_b, l, tm=512, chunk=1280):
    m, k = h.shape
    assert w_in_b.shape[2] - GATE_COL <= GATE_COL
    return pl.pallas_call(
        functools.partial(_in_proj_kernel, chunk=chunk),
        out_shape=jax.ShapeDtypeStruct((m, PROJ_WIDTH), BF16),
        grid=(m // tm,),
        in_specs=[pl.BlockSpec((tm, k), lambda i: (i, 0)),
                  pl.BlockSpec((1, k, GATE_COL), lambda i: (l, 0, 0), pipeline_mode=pl.Buffered(1)),
                  pl.BlockSpec((1, k, GATE_COL), lambda i: (l, 0, 1), pipeline_mode=pl.Buffered(1))],
        out_specs=pl.BlockSpec((tm, PROJ_WIDTH), lambda i: (i, 0)),
        scratch_shapes=[pltpu.VMEM((k, 3 * D_MODEL), BF16)],
        compiler_params=_cparams("arbitrary"),
        name="in_proj",
    )(h, w_in_b, w_in_b)


def _conv_kernel(ac_ref, ab_ref, ah_ref, ba_ref, bg_ref, ga_ref, gb_ref,
                 wa_ref, wb_ref, bb_ref, lng_ref, lnb_ref, wao_ref, wbo_ref,
                 out_ref, ua_ext, ub_ext, ub_sh, vb_scr):
    ts = out_ref.shape[1]
    width = ua_ext.shape[1]

    @pl.when(pl.program_id(1) == 0)
    def _():
        ua_ext[0:HALO, :] = jnp.zeros((HALO, width), F32)
        ub_ext[0:HALO, :] = jnp.zeros((HALO, width), F32)

    ua_ext[HALO:HALO + ts, :] = ac_ref[0].astype(F32) * ah_ref[0].astype(F32)
    ub_ext[HALO:HALO + ts, :] = ba_ref[0].astype(F32) * jax.nn.sigmoid(bg_ref[0].astype(F32))

    wa = wa_ref[...]
    va = wa[0:1] * ua_ext[HALO - 2:HALO - 2 + ts, :]
    va = va + wa[1:2] * ua_ext[HALO - 1:HALO - 1 + ts, :]
    va = va + wa[2:3] * ua_ext[HALO:HALO + ts, :]
    ya = jnp.dot((ab_ref[0].astype(F32) * va).astype(BF16), wao_ref[0], preferred_element_type=F32)

    span = HALO + ts - SUBLANES
    for j in range(1, SUBLANES):
        ub_sh[j - 1, 0:span, :] = ub_ext[j:j + span, :]
    wb = wb_ref[...]
    bias = jnp.broadcast_to(bb_ref[...], (CONV_ROWS, width))
    for c in range(ts // CONV_ROWS):
        acc = bias
        for k in range(CONV_B_K):
            off = HALO - (CONV_B_K - 1) + k
            j = off % SUBLANES
            r0 = off - j + c * CONV_ROWS
            src = ub_ext if j == 0 else ub_sh.at[j - 1]
            acc = acc + wb[k:k + 1] * src[r0:r0 + CONV_ROWS, :]
        vb_scr[c * CONV_ROWS:(c + 1) * CONV_ROWS, :] = acc
    u = vb_scr[...]
    mu = jnp.mean(u, axis=-1, keepdims=True)
    var = jnp.mean(jnp.square(u - mu), axis=-1, keepdims=True)
    y = (u - mu) * lax.rsqrt(var + EPS)
    y = y * lng_ref[...] + lnb_ref[...]
    y = y * jax.nn.sigmoid(y)
    yb = jnp.dot(y.astype(BF16), wbo_ref[0], preferred_element_type=F32)

    out_ref[0] = (jax.nn.sigmoid(ga_ref[0].astype(F32)) * ya
                  + jax.nn.sigmoid(gb_ref[0].astype(F32)) * yb).astype(out_ref.dtype)

    ua_ext[0:HALO, :] = ua_ext[ts:ts + HALO, :]
    ub_ext[0:HALO, :] = ub_ext[ts:ts + HALO, :]


def _conv_mixers(proj, wa, wb, bb, lng, lnb, wao, wbo, l, ts=512):
    b, s, _ = proj.shape
    w = CONV_A_WIDTH
    gate_blk = (GATE_COL + GATE_PAD) // D_MODEL

    def col(width, idx):
        return pl.BlockSpec((1, ts, width), lambda bi, si: (bi, si, idx))

    def full(arr):
        return pl.BlockSpec(arr.shape, lambda bi, si: (0,) * arr.ndim)

    return pl.pallas_call(
        _conv_kernel,
        out_shape=jax.ShapeDtypeStruct((b, s, D_MODEL), BF16),
        grid=(b, s // ts),
        in_specs=[col(w, 0), col(w, 1), col(w, 2), col(w, 3), col(w, 4),
                  col(D_MODEL, gate_blk), col(D_MODEL, gate_blk + 1),
                  full(wa), full(wb), full(bb), full(lng), full(lnb), _layer_spec(wao, l), _layer_spec(wbo, l)],
        out_specs=pl.BlockSpec((1, ts, D_MODEL), lambda bi, si: (bi, si, 0)),
        scratch_shapes=[pltpu.VMEM((HALO + ts, w), F32), pltpu.VMEM((HALO + ts, w), F32),
                        pltpu.VMEM((SUBLANES - 1, HALO + ts - SUBLANES, w), F32),
                        pltpu.VMEM((ts, w), F32)],
        compiler_params=_cparams("parallel", "arbitrary"),
        name="conv_mixers",
    )(proj, proj, proj, proj, proj, proj, proj, wa, wb, bb, lng, lnb, wao, wbo)


def _prep_kernel(q_ref, kv_ref, gt_ref, cos_ref, sin_ref, qg_ref, kg_ref, mblk_ref,
                 qt_ref, kcmp_ref, vcmp_ref, ka_ref, vt_ref, go_ref, stage):
    ts = q_ref.shape[1]
    ck = vt_ref.shape[5]
    cos = cos_ref[...]
    sin = sin_ref[...]
    mblk = mblk_ref[...]
    lane = lax.broadcasted_iota(jnp.int32, (ts, 128), 1)
    first_half = (lane & (HEAD_DIM - 1)) < HEAD_DIM // 2
    low = lane < HEAD_DIM
    tabs = pl.program_id(1) * ts + lax.broadcasted_iota(jnp.int32, (ts, 128), 0)
    blk_onehot = jnp.where(lane - HEAD_DIM == tabs // SLC_BLOCK, 1.0, 0.0)
    ones_rows = jnp.where(lax.broadcasted_iota(jnp.int32, (VT_ROWS - HEAD_DIM, ck), 0) == 0, 1.0, 0.0).astype(BF16)

    def per_group(x):
        return x, pltpu.roll(x, HEAD_DIM, 1)

    def norm_rope(x, g):
        xx = x * x
        hi = xx.astype(BF16)
        lo = (xx - hi.astype(F32)).astype(BF16)
        ms = (jnp.dot(hi, mblk, preferred_element_type=F32)
              + jnp.dot(lo, mblk, preferred_element_type=F32))
        y = x * lax.rsqrt(ms + EPS) * g
        partner = jnp.where(first_half, pltpu.roll(y, 128 - HEAD_DIM // 2, 1), pltpu.roll(y, HEAD_DIM // 2, 1))
        return y * cos + partner * sin

    qg = qg_ref[...]
    for j in range(ATTN_WIDTH // 128):
        o_t = (norm_rope(q_ref[0, :, 128 * j:128 * (j + 1)].astype(F32), qg) * QSCALE).T
        qt_ref[0, 2 * j] = o_t[:HEAD_DIM].astype(BF16)
        qt_ref[0, 2 * j + 1] = o_t[HEAD_DIM:].astype(BF16)
    for i in range(3):
        k = norm_rope(kv_ref[0, :, 256 * i:256 * i + 128].astype(F32), kg_ref[i:i + 1, :])
        v = kv_ref[0, :, 256 * i + 128:256 * i + 256].astype(F32)
        if i == 0:
            for src, dst in ((k, kcmp_ref), (v, vcmp_ref)):
                stage[...] = src
                for lp in range(CMP_STRIDE // 2):
                    even = stage[pl.ds(2 * lp, ts // CMP_STRIDE, stride=CMP_STRIDE), :]
                    odd = stage[pl.ds(2 * lp + 1, ts // CMP_STRIDE, stride=CMP_STRIDE), :]
                    row0 = lax.broadcasted_iota(jnp.int32, even.shape, 1) < HEAD_DIM
                    pair_g0 = jnp.where(row0, even, pltpu.roll(odd, HEAD_DIM, 1))
                    pair_g1 = jnp.where(row0, pltpu.roll(even, HEAD_DIM, 1), odd)
                    dst[0, 0, :, 128 * lp:128 * (lp + 1)] = pair_g0.astype(BF16)
                    dst[0, 1, :, 128 * lp:128 * (lp + 1)] = pair_g1.astype(BF16)
        else:
            extra = blk_onehot if i == 1 else 0.0
            for g, kg in enumerate(per_group(k)):
                ka_ref[0, i - 1, g] = jnp.where(low, kg, extra).astype(BF16)
            v_t = v.T
            for g in range(N_KV):
                for c in range(ts // ck):
                    vt_ref[0, i - 1, g, c, :HEAD_DIM, :] = (
                        v_t[g * HEAD_DIM:(g + 1) * HEAD_DIM, c * ck:(c + 1) * ck].astype(BF16))
                    vt_ref[0, i - 1, g, c, HEAD_DIM:, :] = ones_rows
    sg = jax.nn.sigmoid(gt_ref[0].astype(F32))
    ngate = go_ref.shape[2]
    go_ref[0, 0] = sg.T[:ngate]
    go_ref[0, 1] = pltpu.roll(sg, 128 - GROUP * N_BRANCH, 1).T[:ngate]


def _attn_prep(proj, cos128, sin128, qg128, kg128, mblk, ts=512, ck=ATTN_CHUNK):
    b, s, _ = proj.shape

    def full(arr):
        return pl.BlockSpec(arr.shape, lambda bi, si: (0,) * arr.ndim)

    q_blk = (3 * CONV_A_WIDTH + 2 * CONV_B_WIDTH) // ATTN_WIDTH
    kv_blk = (q_blk + 1) * ATTN_WIDTH // (6 * KV_WIDTH)
    gt_blk = GATE_COL // 128
    return pl.pallas_call(
        _prep_kernel,
        out_shape=(jax.ShapeDtypeStruct((b, N_HEADS, HEAD_DIM, s), BF16),
                   jax.ShapeDtypeStruct((b, N_KV, s // CMP_STRIDE, CMP_STRIDE * HEAD_DIM), BF16),
                   jax.ShapeDtypeStruct((b, N_KV, s // CMP_STRIDE, CMP_STRIDE * HEAD_DIM), BF16),
                   jax.ShapeDtypeStruct((b, 2, N_KV, s, 128), BF16),
                   jax.ShapeDtypeStruct((b, 2, N_KV, s // ck, VT_ROWS, ck), BF16),
                   jax.ShapeDtypeStruct((b, N_KV, GATE_ROWS, s), F32)),
        grid=(b, s // ts),
        in_specs=[pl.BlockSpec((1, ts, ATTN_WIDTH), lambda bi, si: (bi, si, q_blk)),
                  pl.BlockSpec((1, ts, 6 * KV_WIDTH), lambda bi, si: (bi, si, kv_blk)),
                  pl.BlockSpec((1, ts, 128), lambda bi, si: (bi, si, gt_blk)),
                  pl.BlockSpec((ts, 128), lambda bi, si: (si, 0)),
                  pl.BlockSpec((ts, 128), lambda bi, si: (si, 0)),
                  full(qg128), full(kg128), full(mblk)],
        out_specs=(pl.BlockSpec((1, N_HEADS, HEAD_DIM, ts), lambda bi, si: (bi, 0, 0, si)),
                   pl.BlockSpec((1, N_KV, ts // CMP_STRIDE, CMP_STRIDE * HEAD_DIM), lambda bi, si: (bi, 0, si, 0)),
                   pl.BlockSpec((1, N_KV, ts // CMP_STRIDE, CMP_STRIDE * HEAD_DIM), lambda bi, si: (bi, 0, si, 0)),
                   pl.BlockSpec((1, 2, N_KV, ts, 128), lambda bi, si: (bi, 0, 0, si, 0)),
                   pl.BlockSpec((1, 2, N_KV, ts // ck, VT_ROWS, ck), lambda bi, si: (bi, 0, 0, si, 0, 0)),
                   pl.BlockSpec((1, N_KV, GATE_ROWS, ts), lambda bi, si: (bi, 0, 0, si))),
        scratch_shapes=[pltpu.VMEM((ts, 128), F32)],
        compiler_params=_cparams("parallel", "parallel"),
        name="attn_prep",
    )(proj, proj, proj, cos128, sin128, qg128, kg128, mblk)


def _compress_kernel(k_ref, v_ref, pe_ref, w1_ref, b1_ref, w2_ref, kc_ref, vc_ref):
    nb = k_ref.shape[2]
    half = w1_ref.shape[2] // 2
    row = lax.broadcasted_iota(jnp.int32, (nb, w2_ref.shape[2]), 0)
    for i, (src, dst) in enumerate(((k_ref, kc_ref), (v_ref, vc_ref))):
        u = src[0, 0].astype(F32)
        top = (u + pe_ref[i, 0:1, :]).astype(BF16)
        bot = (u + pe_ref[i, 1:2, :]).astype(BF16)
        a = jnp.dot(top, w1_ref[0, i, :half, :], preferred_element_type=F32)
        bm = jnp.dot(bot, w1_ref[0, i, half:, :], preferred_element_type=F32)
        hid = a + pltpu.roll(bm, nb - 1, 0) + b1_ref[i]
        hid = jax.nn.gelu(hid)
        out = jnp.dot(hid.astype(BF16), w2_ref[i], preferred_element_type=F32)
        out = jnp.where(row < nb - 1, out, 0.0)
        dst[0, 0] = (out if i == 0 else out.T).astype(dst.dtype)


def _compress(k4, v4, pe, w1, l, b1, w2):
    b, _, nb, _ = k4.shape
    width = w2.shape[2]
    assert nb == width

    def full(arr):
        return pl.BlockSpec(arr.shape, lambda bi, gi: (0,) * arr.ndim)

    blk = pl.BlockSpec((1, 1, nb, CMP_STRIDE * HEAD_DIM), lambda bi, gi: (bi, gi, 0, 0))
    out_blk = pl.BlockSpec((1, 1, nb, width), lambda bi, gi: (bi, gi, 0, 0))
    return pl.pallas_call(
        _compress_kernel,
        out_shape=(jax.ShapeDtypeStruct((b, N_KV, nb, width), BF16),) * 2,
        grid=(b, N_KV),
        in_specs=[blk, blk, full(pe), _layer_spec(w1, l), full(b1), full(w2)],
        out_specs=(out_blk, out_blk),
        compiler_params=_cparams("parallel", "parallel"),
        name="compress",
    )(k4, v4, pe, w1, b1, w2)


def _attn_kernel(q_ref, kc_ref, vct_ref, ks_ref, vs_ref, kw_ref, vw_ref, gt_ref, o_ref,
                 qa_scr, s_scr, m_scr, acc_scr, oc_scr):
    tq = q_ref.shape[3]
    ck = vs_ref.shape[5]
    s_len = ks_ref.shape[3]
    nc = kc_ref.shape[2]
    n_sel = s_len // SLC_BLOCK
    cols = GROUP * tq
    qi = pl.program_id(2)
    t0 = qi * tq

    def head(r):
        return slice(r * tq, (r + 1) * tq)

    def keys(k_ref, chunk):
        return k_ref[0, 0, 0, pl.ds(pl.multiple_of(chunk * ck, ck), ck), :]

    def gate(branch):
        gt = gt_ref[0, 0]
        return jnp.concatenate([gt[N_BRANCH * r + branch:N_BRANCH * r + branch + 1] for r in range(GROUP)], axis=1)

    q0 = jnp.concatenate([jnp.concatenate([q_ref[0, r] for r in range(GROUP)], axis=1),
                          jnp.zeros((128 - HEAD_DIM, cols), BF16)], axis=0)
    qa_scr[...] = q0

    kj = lax.broadcasted_iota(jnp.int32, (ck, tq), 0)
    qc = lax.broadcasted_iota(jnp.int32, (ck, tq), 1)
    causal = jnp.where(kj <= qc, 0.0, NEG)
    tail = jnp.where(kj > qc, 0.0, NEG)

    def produce(k_ref, chunk, qmat, slot):
        s_scr[slot] = jnp.dot(keys(k_ref, chunk), qmat, preferred_element_type=F32)

    def consume(slot, v_ref, chunk, mask):
        s = s_scr[slot]
        if mask is not None:
            s = s + jnp.tile(mask, (1, GROUP))
        m_prev = m_scr[...]
        m_new = jnp.maximum(m_prev, jnp.max(s, axis=0, keepdims=True))
        alpha = jnp.exp2(m_prev - m_new)
        p = jnp.exp2(s - m_new).astype(BF16)
        acc_scr[...] = alpha * acc_scr[...] + jnp.dot(v_ref[0, 0, 0, chunk], p, preferred_element_type=F32)
        m_scr[...] = m_new

    def reset():
        m_scr[...] = jnp.full(m_scr.shape, NEG, F32)
        acc_scr[...] = jnp.zeros(acc_scr.shape, F32)

    def finish():
        acc = acc_scr[...]
        return acc[:HEAD_DIM] / acc[HEAD_DIM:HEAD_DIM + 1]

    n_win = WINDOW // ck
    win_chunks = [(qi, causal)]
    for d in range(1, n_win + 1):
        base = tail if d == n_win else jnp.zeros((ck, tq), F32)
        win_chunks.append((jnp.maximum(qi - d, 0), jnp.where(qi >= d, base, NEG)))
    win_slot0 = 2
    for i, (c, _) in enumerate(win_chunks):
        produce(kw_ref, c, q0, win_slot0 + i)

    tpos = t0 + (lax.broadcasted_iota(jnp.int32, (nc, cols), 1) & (tq - 1))
    cidx = lax.broadcasted_iota(jnp.int32, (nc, cols), 0)
    cmask = ((cidx * CMP_STRIDE + (CMP_BLOCK - 1)) <= tpos) & (cidx < nc - 1)
    sc = jnp.dot(kc_ref[0, 0], q0, preferred_element_type=F32)
    sc = jnp.where(cmask, sc, NEG)
    mc = jnp.max(sc, axis=0, keepdims=True)
    ec = jnp.where(cmask, jnp.exp2(sc - mc), 0.0)
    dc = jnp.sum(ec, axis=0, keepdims=True)
    pc = ec / jnp.where(dc > 0.0, dc, 1.0)
    o_cmp = jnp.dot(vct_ref[0, 0], pc.astype(BF16), preferred_element_type=F32)

    psum = pc[:, head(0)] + pc[:, head(1)] + pc[:, head(2)] + pc[:, head(3)]
    p_hi = psum.astype(BF16)
    p_lo = (psum - p_hi.astype(F32)).astype(BF16)
    jrow = lax.broadcasted_iota(jnp.int32, (n_sel, nc), 0)
    ccol = lax.broadcasted_iota(jnp.int32, (n_sel, nc), 1)
    ovl = ((ccol * CMP_STRIDE < jrow * SLC_BLOCK + SLC_BLOCK)
           & (ccol * CMP_STRIDE + CMP_BLOCK > jrow * SLC_BLOCK))
    ovl = jnp.where(ovl, 1.0, 0.0).astype(BF16)
    imp = (jnp.dot(ovl, p_hi, preferred_element_type=F32)
           + jnp.dot(ovl, p_lo, preferred_element_type=F32))

    jj = lax.broadcasted_iota(jnp.int32, (n_sel, tq), 0)
    cur = (t0 + lax.broadcasted_iota(jnp.int32, (n_sel, tq), 1)) // SLC_BLOCK
    valid = jj <= cur
    forced = (jj == 0) | (jj == cur) | (jj == cur - 1)
    score = jnp.where(forced, jnp.inf, jnp.where(valid, imp, -jnp.inf))
    rank = jnp.zeros((n_sel, tq), F32)
    for jp in range(n_sel):
        other = score[jp:jp + 1, :]
        ahead = (other > score) | ((other == score) & (jj > jp))
        rank = rank + jnp.where(ahead, 1.0, 0.0)
    sel = jnp.where(valid & (rank < float(min(SLC_TOPN, n_sel))), 1.0, 0.0)

    pen = jnp.where(sel > 0.5, 0.0, NEG).astype(BF16)
    for r in range(GROUP):
        qa_scr[HEAD_DIM:HEAD_DIM + n_sel, head(r)] = pen

    produce(ks_ref, 0, qa_scr[...], 0)

    reset()
    for i, (c, msk) in enumerate(win_chunks):
        consume(win_slot0 + i, vw_ref, c, msk)
    oc_scr[...] = gate(0) * o_cmp[:HEAD_DIM] + gate(2) * finish()

    reset()

    def pair_body(j, carry):
        produce(ks_ref, 2 * j + 1, qa_scr[...], 1)
        consume(0, vs_ref, 2 * j, None)
        produce(ks_ref, 2 * j + 2, qa_scr[...], 0)
        consume(1, vs_ref, 2 * j + 1, None)
        return carry

    lax.fori_loop(0, qi // 2, pair_body, 0)

    @pl.when(qi % 2 == 1)
    def _():
        produce(ks_ref, qi, qa_scr[...], 1)
        consume(0, vs_ref, qi - 1, None)
        consume(1, vs_ref, qi, causal)

    @pl.when(qi % 2 == 0)
    def _():
        consume(0, vs_ref, qi, causal)

    o_slc = finish()

    o_all = oc_scr[...] + gate(1) * o_slc
    o_ref[0] = jnp.concatenate([o_all[:, head(r)] for r in range(GROUP)], axis=0).T.astype(o_ref.dtype)


def _attention(qt, kc, vct, ka, vt, gates_t, tq=ATTN_CHUNK):
    b, _, _, s = qt.shape
    nc = kc.shape[2]
    ck = vt.shape[5]
    cols = GROUP * tq
    assert tq == ck and WINDOW % ck == 0 and tq & (tq - 1) == 0

    kseq = lambda branch: pl.BlockSpec((1, 1, 1, s, 128), lambda bi, gi, qi: (bi, branch, gi, 0, 0))
    vseq = lambda branch: pl.BlockSpec((1, 1, 1, s // ck, VT_ROWS, ck), lambda bi, gi, qi: (bi, branch, gi, 0, 0, 0))
    cblk = pl.BlockSpec((1, 1, nc, 128), lambda bi, gi, qi: (bi, gi, 0, 0))
    return pl.pallas_call(
        _attn_kernel,
        out_shape=jax.ShapeDtypeStruct((b, s, ATTN_WIDTH), BF16),
        grid=(b, N_KV, s // tq),
        in_specs=[pl.BlockSpec((1, GROUP, HEAD_DIM, tq), lambda bi, gi, qi: (bi, gi, 0, qi)),
                  cblk, cblk, kseq(0), vseq(0), kseq(1), vseq(1),
                  pl.BlockSpec((1, 1, GATE_ROWS, tq), lambda bi, gi, qi: (bi, gi, 0, qi))],
        out_specs=pl.BlockSpec((1, tq, GROUP * HEAD_DIM), lambda bi, gi, qi: (bi, qi, gi)),
        scratch_shapes=[pltpu.VMEM((128, cols), BF16),
                        pltpu.VMEM((WINDOW // ck + 3, ck, cols), F32),
                        pltpu.VMEM((1, cols), F32),
                        pltpu.VMEM((VT_ROWS, cols), F32),
                        pltpu.VMEM((HEAD_DIM, cols), F32)],
        compiler_params=_cparams("parallel", "parallel", "arbitrary"),
        name="nsa_attention",
    )(qt, kc, vct, ka, vt, ka, vt, gates_t)


def _rms(x, g):
    ms = jnp.mean(x * x, axis=-1, keepdims=True)
    return x * lax.rsqrt(ms + EPS) * g


def _merge_ffn_kernel(o_ref, mab_ref, gc_ref, x_ref, wn_ref, wo_ref, g2_ref, w13_ref, w2_ref, gn_ref,
                      xo_ref, hn_ref, *, bounds):
    yc = jnp.dot(o_ref[...], wn_ref[0], preferred_element_type=F32)
    mixed = mab_ref[...].astype(F32) + jax.nn.sigmoid(gc_ref[...].astype(F32)) * yc
    x = x_ref[...] + jnp.dot(mixed.astype(BF16), wo_ref[0], preferred_element_type=F32)
    h = _rms(x, g2_ref[...]).astype(BF16)
    for lo, hi in bounds:
        u = jnp.dot(h, w13_ref[0, :, lo:hi], preferred_element_type=F32)
        v = jnp.dot(h, w13_ref[0, :, D_FF + lo:D_FF + hi], preferred_element_type=F32)
        act = (u * jax.nn.sigmoid(u) * v).astype(BF16)
        x = x + jnp.dot(act, w2_ref[0, lo:hi, :], preferred_element_type=F32)
    xo_ref[...] = x
    hn_ref[...] = _rms(x, gn_ref[...]).astype(hn_ref.dtype)


def _merge_ffn(o, mab, proj2d, x, wn, wo, g2, w13, w2, l, g_next, tm=512, n_chunks=2):
    t, d = x.shape
    gc_blk = (GATE_COL + GATE_PAD) // D_MODEL + 2
    tiles = D_FF // MXU_COLS
    edges = [MXU_COLS * (tiles * i // n_chunks) for i in range(n_chunks + 1)]
    row = lambda width, col=0: pl.BlockSpec((tm, width), lambda i: (i, col))
    vec = pl.BlockSpec((1, d), lambda i: (0, 0))
    return pl.pallas_call(
        functools.partial(_merge_ffn_kernel, bounds=tuple(zip(edges[:-1], edges[1:]))),
        out_shape=(jax.ShapeDtypeStruct((t, d), F32), jax.ShapeDtypeStruct((t, d), BF16)),
        grid=(t // tm,),
        in_specs=[row(ATTN_WIDTH), row(d), row(d, gc_blk), row(d),
                  _layer_spec(wn, l, resident=True), _layer_spec(wo, l, resident=True), vec,
                  _layer_spec(w13, l, resident=True), _layer_spec(w2, l, resident=True), vec],
        out_specs=(row(d), row(d)),
        compiler_params=_cparams("parallel"),
        name="merge_ffn",
    )(o, mab, proj2d, x, wn, wo, g2, w13, w2, g_next)


def _rope_tables(s):
    pos = jnp.arange(s, dtype=F32)
    inv = 1.0 / (ROPE_THETA ** (jnp.arange(0, HEAD_DIM, 2, dtype=F32) / HEAD_DIM))
    ang = pos[:, None] * inv[None, :]
    cos, sin = jnp.cos(ang), jnp.sin(ang)
    cos128 = jnp.concatenate([cos, cos, cos, cos], axis=-1)
    sin128 = jnp.concatenate([-sin, sin, -sin, sin], axis=-1)
    return cos128, sin128


def kernel(x, norm1_g, w_in, a_conv_w, a_w_out, b_conv_w, b_conv_b, b_ln_g, b_ln_b, b_w_out,
           q_norm_g, k_norm_g, cmp_pe, cmp_w1, cmp_b1, cmp_w2, nsa_w_out, w_o, norm2_g,
           ffn_w13, ffn_w2):
    b, s, d = x.shape
    depth = w_in.shape[0]
    t = b * s
    cos128, sin128 = _rope_tables(s)
    hid = jnp.arange(128) // HEAD_DIM
    mblk = jnp.where(hid[:, None] == hid[None, :], 1.0 / HEAD_DIM, 0.0).astype(BF16)

    w_in_b = w_in.astype(BF16)
    a_w_out_b = a_w_out.astype(BF16)
    b_w_out_b = b_w_out.astype(BF16)
    cmp_w1_b = cmp_w1.astype(BF16)
    cmp_w2_b = jnp.pad(cmp_w2, ((0, 0), (0, 0), (0, 0), (0, 128 - HEAD_DIM))).astype(BF16)
    nsa_w_out_b = nsa_w_out.astype(BF16)
    w_o_b = w_o.astype(BF16)
    ffn_w13_b = ffn_w13.astype(BF16)
    ffn_w2_b = ffn_w2.astype(BF16)
    qg128 = jnp.concatenate([q_norm_g, q_norm_g], axis=-1)[:, None, :]
    kg128 = jnp.concatenate([k_norm_g, k_norm_g], axis=-1)
    pe2 = cmp_pe.reshape(depth, 2, 2, (CMP_BLOCK // 2) * HEAD_DIM)
    b1 = cmp_b1[:, :, None, :]

    xf = x.reshape(t, d)
    h = _rmsnorm(xf, norm1_g[0:1])
    for l in range(depth):
        proj = _in_proj(h, w_in_b, l)
        proj3 = proj.reshape(b, s, PROJ_WIDTH)
        mab = _conv_mixers(proj3, a_conv_w[l], b_conv_w[l], b_conv_b[l:l + 1], b_ln_g[l:l + 1],
                           b_ln_b[l:l + 1], a_w_out_b, b_w_out_b, l)
        qt, kcmp, vcmp, ka, vt, gates_t = _attn_prep(proj3, cos128, sin128, qg128[l], kg128[l], mblk)
        kc, vct = _compress(kcmp, vcmp, pe2[l], cmp_w1_b, l, b1[l], cmp_w2_b[l])
        o = _attention(qt, kc, vct, ka, vt, gates_t)
        g_next = norm1_g[l + 1:l + 2] if l + 1 < depth else norm1_g[l:l + 1]
        xf, h = _merge_ffn(o.reshape(t, ATTN_WIDTH), mab.reshape(t, d), proj, xf, nsa_w_out_b, w_o_b,
                           norm2_g[l:l + 1], ffn_w13_b, ffn_w2_b, l, g_next)
    return xf.reshape(b, s, d)
```
